```python
import math
import jax, jax.numpy as jnp
from jax import lax
import numpy as np

D_MODEL = 1024
BATCH = 4
SEQ = 8192
DEPTH = 1
DEC_BATCH = 8
DEC_SEQ = 2048
PAST_LEN = 128

HEAD_DIM = 64
A_HEADS = 4
A_VDIM = 2 * HEAD_DIM
A_WIDTH = A_HEADS * A_VDIM
A_Q_COLS = A_HEADS * 2 * HEAD_DIM
A_K_COLS = A_HEADS * 2 * HEAD_DIM
A_V_COLS = A_HEADS * A_VDIM
B_HEADS = 8
B_KV_HEADS = 2
B_GROUP = B_HEADS // B_KV_HEADS
B_WIDTH = B_HEADS * HEAD_DIM
B_Q_COLS = B_HEADS * HEAD_DIM
B_K_COLS = B_KV_HEADS * HEAD_DIM
B_V_COLS = B_KV_HEADS * HEAD_DIM
MIX_WIDTH = A_WIDTH + B_WIDTH
IN_COLS = A_Q_COLS + A_K_COLS + A_V_COLS + B_Q_COLS + B_K_COLS + B_V_COLS
D_FF = 4 * D_MODEL
PLE_DIM = 256
GRID_W = 64
Q_BLOCK = 128
ROPE_THETA = 10000.0
ROPE_HALF = HEAD_DIM // 2
NORM_EPS = 1e-6
SUBLN_EPS = 1e-5

kernel_name = "hymba_diffattn_axialgqa_encoder"


def _rms_norm(x, g, eps):
    xf = x.astype(jnp.float32)
    y = xf * lax.rsqrt(jnp.mean(xf * xf, axis=-1, keepdims=True) + eps)
    return (y * g.astype(jnp.float32)).astype(x.dtype)


def _alibi_slopes():
    return jnp.asarray([2.0 ** (-8.0 * (h + 1) / A_HEADS) for h in range(A_HEADS)], dtype=jnp.float32)


def _lambda_init(layer_idx):
    return 0.8 - 0.6 * math.exp(-0.3 * layer_idx)


def _to_blocks(t):
    b, s = t.shape[0], t.shape[1]
    return jnp.moveaxis(t.reshape((b, s // Q_BLOCK, Q_BLOCK) + t.shape[2:]), 1, 0)


def _from_blocks(t):
    t = jnp.moveaxis(t, 0, 1)
    return t.reshape((t.shape[0], t.shape[1] * t.shape[2]) + t.shape[3:])


def _diff_attention(q, k, v, lam, slopes):
    s_len = q.shape[1]
    scale = HEAD_DIM ** -0.5
    kpos = jnp.arange(s_len)
    nblk = s_len // Q_BLOCK

    def blk(args):
        qb, i = args
        qpos = i * Q_BLOCK + jnp.arange(Q_BLOCK)
        dist = jnp.abs(qpos[:, None] - kpos[None, :]).astype(jnp.float32)
        sc = jnp.einsum('bqhcd,bkhcd->bhcqk', qb, k) * scale
        sc = sc - slopes[None, :, None, None, None] * dist[None, None, None]
        pr = jax.nn.softmax(sc, axis=-1)
        w = pr[:, :, 0] - lam * pr[:, :, 1]
        return jnp.einsum('bhqk,bkhe->bqhe', w, v)

    out = lax.map(blk, (_to_blocks(q), jnp.arange(nblk)))
    return _from_blocks(out)


def _gqa_attention(q, k, v):
    scale = HEAD_DIM ** -0.5

    def blk(qb):
        sc = jnp.einsum('bqngd,bknd->bngqk', qb, k) * scale
        pr = jax.nn.softmax(sc, axis=-1)
        return jnp.einsum('bngqk,bknd->bqngd', pr, v)

    return _from_blocks(lax.map(blk, _to_blocks(q)))


def _rope_1d(x, ang):
    f = ang.shape[-1]
    x1, x2 = x[..., :f], x[..., f:]
    c = jnp.cos(ang)[:, None, :]
    s = jnp.sin(ang)[:, None, :]
    return jnp.concatenate([x1 * c - x2 * s, x1 * s + x2 * c], axis=-1)


def _axial_angles(s_len):
    rows = s_len // GRID_W
    row_idx = jnp.repeat(jnp.arange(rows), GRID_W).astype(jnp.float32)
    col_idx = jnp.tile(jnp.arange(GRID_W), rows).astype(jnp.float32)
    inv_freq = ROPE_THETA ** (-jnp.arange(0, ROPE_HALF, 2, dtype=jnp.float32) / ROPE_HALF)
    return row_idx[:, None] * inv_freq[None], col_idx[:, None] * inv_freq[None]


def _axial_rope(x, ang_r, ang_c):
    return jnp.concatenate([_rope_1d(x[..., :ROPE_HALF], ang_r), _rope_1d(x[..., ROPE_HALF:], ang_c)], axis=-1)


def _layer(h, p_l, layer_idx, w_in, g_mix, lambda_q1, lambda_k1, lambda_q2, lambda_k2, g_subln,
           g_qnorm, g_knorm, w_out, g_mlp, w_ff1, w_ff2, g_ple, w_ple_gate, w_ple_proj):
    bsz, s_len, _ = h.shape
    f32 = jnp.float32
    n = _rms_norm(h, g_mix, NORM_EPS)
    z = jnp.einsum('bsd,dc->bsc', n, w_in).astype(f32)
    cuts = np.cumsum([A_Q_COLS, A_K_COLS, A_V_COLS, B_Q_COLS, B_K_COLS]).tolist()
    qa, ka, va, qb, kb, vb = jnp.split(z, cuts, axis=-1)

    lam_init = _lambda_init(layer_idx)
    lam = (jnp.exp(jnp.sum(lambda_q1.astype(f32) * lambda_k1.astype(f32)))
           - jnp.exp(jnp.sum(lambda_q2.astype(f32) * lambda_k2.astype(f32))) + lam_init)
    qa = qa.reshape(bsz, s_len, A_HEADS, 2, HEAD_DIM)
    ka = ka.reshape(bsz, s_len, A_HEADS, 2, HEAD_DIM)
    va = va.reshape(bsz, s_len, A_HEADS, A_VDIM)
    oa = _diff_attention(qa, ka, va, lam, _alibi_slopes())
    oa = _rms_norm(oa, g_subln, SUBLN_EPS) * (1.0 - lam_init)
    oa = oa.reshape(bsz, s_len, A_WIDTH)

    ang_r, ang_c = _axial_angles(s_len)
    qb = _rms_norm(qb.reshape(bsz, s_len, B_HEADS, HEAD_DIM), g_qnorm, NORM_EPS)
    kb = _rms_norm(kb.reshape(bsz, s_len, B_KV_HEADS, HEAD_DIM), g_knorm, NORM_EPS)
    qb = _axial_rope(qb, ang_r, ang_c).reshape(bsz, s_len, B_KV_HEADS, B_GROUP, HEAD_DIM)
    kb = _axial_rope(kb, ang_r, ang_c)
    vb = vb.reshape(bsz, s_len, B_KV_HEADS, HEAD_DIM)
    ob = _gqa_attention(qb, kb, vb).reshape(bsz, s_len, B_WIDTH)

    mixed = jnp.concatenate([oa, ob], axis=-1).astype(h.dtype)
    h = h + jnp.einsum('bsc,cd->bsd', mixed, w_out)

    n2 = _rms_norm(h, g_mlp, NORM_EPS)
    a = jax.nn.relu(jnp.einsum('bsd,df->bsf', n2, w_ff1))
    h = h + jnp.einsum('bsf,fd->bsd', a * a, w_ff2)

    gate = jax.nn.sigmoid(jnp.einsum('bsd,de->bse', _rms_norm(h, g_ple, NORM_EPS), w_ple_gate))
    h = h + gate * jnp.einsum('bsp,pd->bsd', p_l, w_ple_proj)
    return h


def _trunk(x, p, w_in, g_mix, lambda_q1, lambda_k1, lambda_q2, lambda_k2, g_subln, g_qnorm, g_knorm,
           w_out, g_mlp, w_ff1, w_ff2, g_ple, w_ple_gate, w_ple_proj, g_final):
    h = x
    for l in range(DEPTH):
        h = _layer(h, p[l], l, w_in[l], g_mix[l], lambda_q1[l], lambda_k1[l], lambda_q2[l], lambda_k2[l],
                   g_subln[l], g_qnorm[l], g_knorm[l], w_out[l], g_mlp[l], w_ff1[l], w_ff2[l],
                   g_ple[l], w_ple_gate[l], w_ple_proj[l])
    return _rms_norm(h, g_final, NORM_EPS)


def setup_inputs(seed: int = 0) -> dict:
    key = jax.random.key(seed)
    ks = jax.random.split(key, 24)
    f32 = jnp.float32

    def nrm(k, shape, scale):
        return jax.random.normal(k, shape, f32) * scale

    def gain(k, shape):
        return 1.0 + 0.02 * jax.random.normal(k, shape, f32)

    return {
        "x_prompt": nrm(ks[0], (BATCH, SEQ, D_MODEL), 1.0),
        "x_sample": nrm(ks[1], (DEC_BATCH, DEC_SEQ, D_MODEL), 1.0),
        "p_prompt": nrm(ks[2], (DEPTH, BATCH, SEQ, PLE_DIM), 1.0),
        "p_sample": nrm(ks[3], (DEPTH, DEC_BATCH, DEC_SEQ, PLE_DIM), 1.0),
        "w_in": nrm(ks[4], (DEPTH, D_MODEL, IN_COLS), D_MODEL ** -0.5),
        "g_mix": gain(ks[5], (DEPTH, D_MODEL)),
        "lambda_q1": nrm(ks[6], (DEPTH, HEAD_DIM), 0.1),
        "lambda_k1": nrm(ks[7], (DEPTH, HEAD_DIM), 0.1),
        "lambda_q2": nrm(ks[8], (DEPTH, HEAD_DIM), 0.1),
        "lambda_k2": nrm(ks[9], (DEPTH, HEAD_DIM), 0.1),
        "g_subln": gain(ks[10], (DEPTH, A_VDIM)),
        "g_qnorm": gain(ks[11], (DEPTH, HEAD_DIM)),
        "g_knorm": gain(ks[12], (DEPTH, HEAD_DIM)),
        "w_out": nrm(ks[13], (DEPTH, MIX_WIDTH, D_MODEL), MIX_WIDTH ** -0.5),
        "g_mlp": gain(ks[14], (DEPTH, D_MODEL)),
        "w_ff1": nrm(ks[15], (DEPTH, D_MODEL, D_FF), D_MODEL ** -0.5),
        "w_ff2": nrm(ks[16], (DEPTH, D_FF, D_MODEL), D_FF ** -0.5),
        "g_ple": gain(ks[17], (DEPTH, D_MODEL)),
        "w_ple_gate": nrm(ks[18], (DEPTH, D_MODEL, D_MODEL), D_MODEL ** -0.5),
        "w_ple_proj": nrm(ks[19], (DEPTH, PLE_DIM, D_MODEL), PLE_DIM ** -0.5),
        "g_final": gain(ks[20], (D_MODEL,)),
    }


def reference(x_prompt, x_sample, p_prompt, p_sample, w_in, g_mix, lambda_q1, lambda_k1, lambda_q2, lambda_k2,
              g_subln, g_qnorm, g_knorm, w_out, g_mlp, w_ff1, w_ff2, g_ple, w_ple_gate, w_ple_proj, g_final):
    y_prompt = _trunk(x_prompt, p_prompt, w_in, g_mix, lambda_q1, lambda_k1, lambda_q2, lambda_k2, g_subln,
                      g_qnorm, g_knorm, w_out, g_mlp, w_ff1, w_ff2, g_ple, w_ple_gate, w_ple_proj, g_final)
    y_sample = _trunk(x_sample, p_sample, w_in, g_mix, lambda_q1, lambda_k1, lambda_q2, lambda_k2, g_subln,
                      g_qnorm, g_knorm, w_out, g_mlp, w_ff1, w_ff2, g_ple, w_ple_gate, w_ple_proj, g_final)
    return (y_prompt, y_sample)
```

```python
import functools
import math

import jax
import jax.numpy as jnp
import numpy as np
from jax import lax
from jax.experimental import pallas as pl
from jax.experimental.pallas import tpu as pltpu

D_MODEL = 1024
HEAD_DIM = 64
A_HEADS = 4
A_VDIM = 2 * HEAD_DIM
A_WIDTH = A_HEADS * A_VDIM
B_HEADS = 8
B_KV_HEADS = 2
B_GROUP = B_HEADS // B_KV_HEADS
B_WIDTH = B_HEADS * HEAD_DIM
B_KV_WIDTH = B_KV_HEADS * HEAD_DIM
IN_COLS = 3 * A_WIDTH + B_WIDTH + 2 * B_KV_WIDTH
D_FF = 4 * D_MODEL
PLE_DIM = 256
GRID_W = 64
ROPE_THETA = 10000.0
ROPE_HALF = HEAD_DIM // 2
ROPE_QUARTER = ROPE_HALF // 2
NORM_EPS = 1e-6
SUBLN_EPS = 1e-5
SCALE = HEAD_DIM ** -0.5

LANES = 128
POS_SPLIT = 128
NEG_BIG = -1e30

TM_PROJ = 512
TM_POST = 256
TQ_A = 256
TK_A = 512
TQ_B = 256
TK_B = 256
FF_CHUNK = 1024
VMEM_LIMIT = 56 * 1024 * 1024

BF16 = jnp.bfloat16
F32 = jnp.float32
NT_DIMS = (((1,), (1,)), ((), ()))


def _lambda_init(layer_idx):
    return 0.8 - 0.6 * math.exp(-0.3 * layer_idx)


def _rms(x, g, eps):
    return x * lax.rsqrt(jnp.mean(x * x, axis=-1, keepdims=True) + eps) * g


def _group_sumsq(y, ones):
    y2 = y * y
    hi = y2.astype(BF16)
    lo = (y2 - hi.astype(F32)).astype(BF16)
    return jnp.dot(jnp.concatenate([hi, lo], axis=1), jnp.concatenate([ones, ones], axis=0),
                   preferred_element_type=F32)


def _rope_chunk(y, cos, sin_lo, sin_hi):
    return (y * cos
            + pltpu.roll(y, LANES - ROPE_QUARTER, 1) * sin_lo
            + pltpu.roll(y, ROPE_QUARTER, 1) * sin_hi)


def _proj_kernel(x_ref, gmix_ref, win_ref, ones_ref, gq_ref, gk_ref, cos_ref, slo_ref, shi_ref,
                 qa_ref, ka_ref, vat_ref, qb_ref, kb_ref, vbt_ref, *, tka, tkb):
    x = x_ref[...]
    n = _rms(x, gmix_ref[...], NORM_EPS)
    z = jnp.dot(n.astype(BF16), win_ref[...], preferred_element_type=F32)
    tm = x.shape[0]

    c0 = 0
    qa_ref[...] = (z[:, c0:c0 + A_WIDTH] * SCALE).astype(BF16)
    c0 += A_WIDTH
    ka_ref[...] = z[:, c0:c0 + A_WIDTH].astype(BF16)
    c0 += A_WIDTH
    va = z[:, c0:c0 + A_WIDTH]
    c0 += A_WIDTH
    qb = z[:, c0:c0 + B_WIDTH]
    c0 += B_WIDTH
    kb = z[:, c0:c0 + B_KV_WIDTH]
    c0 += B_KV_WIDTH
    vb = z[:, c0:c0 + B_KV_WIDTH]

    for j in range(tm // tka):
        vat_ref[0, j] = va[j * tka:(j + 1) * tka, :].T.astype(BF16)
    for j in range(tm // tkb):
        vbt_ref[0, j] = vb[j * tkb:(j + 1) * tkb, :].T.astype(BF16)

    ones = ones_ref[...]
    cos = cos_ref[...]
    slo = slo_ref[...]
    shi = shi_ref[...]

    qn = qb * lax.rsqrt(_group_sumsq(qb, ones) * (1.0 / HEAD_DIM) + NORM_EPS) * gq_ref[...]
    for j in range(B_WIDTH // LANES):
        rot = _rope_chunk(qn[:, j * LANES:(j + 1) * LANES], cos, slo, shi) * SCALE
        qb_ref[2 * j] = rot[:, :HEAD_DIM].astype(BF16)
        qb_ref[2 * j + 1] = rot[:, HEAD_DIM:].astype(BF16)

    kn = kb * lax.rsqrt(_group_sumsq(kb, ones[:B_KV_WIDTH, :B_KV_WIDTH]) * (1.0 / HEAD_DIM)
                        + NORM_EPS) * gk_ref[...]
    rot = _rope_chunk(kn, cos, slo, shi)
    kb_ref[0] = rot[:, :HEAD_DIM].astype(BF16)
    kb_ref[1] = rot[:, HEAD_DIM:].astype(BF16)


def _rope_tables(s_len):
    t = jnp.arange(s_len)
    inv_freq = ROPE_THETA ** (-jnp.arange(0, ROPE_HALF, 2, dtype=F32) / ROPE_HALF)
    ang_r = (t // GRID_W).astype(F32)[:, None] * inv_freq[None]
    ang_c = (t % GRID_W).astype(F32)[:, None] * inv_freq[None]
    zero = jnp.zeros_like(ang_r)
    cos = jnp.concatenate([jnp.cos(ang_r)] * 2 + [jnp.cos(ang_c)] * 2, axis=-1)
    sin_lo = jnp.concatenate([-jnp.sin(ang_r), zero, -jnp.sin(ang_c), zero], axis=-1)
    sin_hi = jnp.concatenate([zero, jnp.sin(ang_r), zero, jnp.sin(ang_c)], axis=-1)
    rep = LANES // HEAD_DIM
    return tuple(jnp.tile(a, (1, rep)) for a in (cos, sin_lo, sin_hi))


def _project(x2d, bsz, s_len, g_mix, w_in, g_qnorm, g_knorm):
    t_total = x2d.shape[0]
    tm = TM_PROJ
    tiles_per_b = s_len // tm
    cos, slo, shi = _rope_tables(s_len)
    grp = jnp.arange(B_WIDTH) // HEAD_DIM
    ones = (grp[:, None] == grp[None, :]).astype(BF16)
    gq = jnp.tile(g_qnorm.astype(F32), B_HEADS)[None]
    gk = jnp.tile(g_knorm.astype(F32), B_KV_HEADS)[None]

    const = lambda t: (0, 0)
    out_shape = (
        jax.ShapeDtypeStruct((t_total, A_WIDTH), BF16),
        jax.ShapeDtypeStruct((t_total, A_WIDTH), BF16),
        jax.ShapeDtypeStruct((bsz, s_len // TK_A, A_WIDTH, TK_A), BF16),
        jax.ShapeDtypeStruct((B_HEADS, t_total, HEAD_DIM), BF16),
        jax.ShapeDtypeStruct((B_KV_HEADS, t_total, HEAD_DIM), BF16),
        jax.ShapeDtypeStruct((bsz, s_len // TK_B, B_KV_WIDTH, TK_B), BF16),
    )
    return pl.pallas_call(
        functools.partial(_proj_kernel, tka=TK_A, tkb=TK_B),
        grid=(t_total // tm,),
        in_specs=[
            pl.BlockSpec((tm, D_MODEL), lambda t: (t, 0)),
            pl.BlockSpec((1, D_MODEL), const),
            pl.BlockSpec((D_MODEL, IN_COLS), const),
            pl.BlockSpec((B_WIDTH, B_WIDTH), const),
            pl.BlockSpec((1, B_WIDTH), const),
            pl.BlockSpec((1, B_KV_WIDTH), const),
            pl.BlockSpec((tm, LANES), lambda t: (t % tiles_per_b, 0)),
            pl.BlockSpec((tm, LANES), lambda t: (t % tiles_per_b, 0)),
            pl.BlockSpec((tm, LANES), lambda t: (t % tiles_per_b, 0)),
        ],
        out_specs=(
            pl.BlockSpec((tm, A_WIDTH), lambda t: (t, 0)),
            pl.BlockSpec((tm, A_WIDTH), lambda t: (t, 0)),
            pl.BlockSpec((1, tm // TK_A, A_WIDTH, TK_A),
                         lambda t: (t // tiles_per_b, t % tiles_per_b, 0, 0)),
            pl.BlockSpec((B_HEADS, tm, HEAD_DIM), lambda t: (0, t, 0)),
            pl.BlockSpec((B_KV_HEADS, tm, HEAD_DIM), lambda t: (0, t, 0)),
            pl.BlockSpec((1, tm // TK_B, B_KV_WIDTH, TK_B),
                         lambda t: (t // tiles_per_b, t % tiles_per_b, 0, 0)),
        ),
        out_shape=out_shape,
        compiler_params=pltpu.CompilerParams(
            dimension_semantics=("parallel",), vmem_limit_bytes=VMEM_LIMIT),
        name="proj",
    )(x2d, g_mix[None].astype(F32), w_in.astype(BF16), ones, gq, gk, cos, slo, shi)


def _softmax_step(s, vt, m_ref, l_ref, acc_ref, idx):
    m_old = m_ref[idx]
    m_new = jnp.maximum(m_old, jnp.max(s, axis=0, keepdims=True))
    alpha = jnp.exp(m_old - m_new)
    p = jnp.exp(s - m_new)
    l_ref[idx] = alpha * l_ref[idx] + jnp.sum(p, axis=0, keepdims=True)
    acc_ref[idx] = alpha * acc_ref[idx] + jnp.dot(vt, p.astype(BF16), preferred_element_type=F32)
    m_ref[idx] = m_new


def _attn_a_kernel(lq1_ref, lk1_ref, lq2_ref, lk2_ref, slope_ref, g_ref, q_ref, qaug_ref, k_ref,
                   kaug_ref, vt_ref, o_ref, m_ref, l_ref, acc_ref, *, lam_init):
    tq = q_ref.shape[0]
    tk = vt_ref.shape[-1]
    nk = vt_ref.shape[1]
    q0 = pl.program_id(2) * tq

    q = q_ref[...]
    lane = lax.broadcasted_iota(jnp.int32, q.shape, 1)
    zero = jnp.zeros_like(q)
    qmaps = (jnp.where(lane < HEAD_DIM, q, zero), jnp.where(lane >= HEAD_DIM, q, zero))
    qaug = qaug_ref[0]
    w_left = [jnp.concatenate([qm, qaug], axis=1) for qm in qmaps]
    w_right = [jnp.concatenate([qm, -qaug], axis=1) for qm in qmaps]

    m_ref[...] = jnp.full(m_ref.shape, NEG_BIG, F32)
    l_ref[...] = jnp.zeros(l_ref.shape, F32)
    acc_ref[...] = jnp.zeros(acc_ref.shape, F32)

    def tile(kt, weights, fix):
        k0 = pl.multiple_of(kt * tk, tk)
        kcat = jnp.concatenate([k_ref[pl.ds(k0, tk), :], kaug_ref[0, pl.ds(k0, tk), :]], axis=1)
        vt = vt_ref[0, kt]
        for c in range(2):
            s = lax.dot_general(kcat, weights[c], NT_DIMS, preferred_element_type=F32)
            if fix is not None:
                s = s + fix(k0)
            _softmax_step(s, vt, m_ref, l_ref, acc_ref, c)

    def diag_fix(k0):
        kpos = k0 + lax.broadcasted_iota(jnp.int32, (tk, tq), 0)
        qpos = q0 + lax.broadcasted_iota(jnp.int32, (tk, tq), 1)
        over = jnp.maximum(kpos - qpos, 0).astype(F32)
        return over * (-2.0 * slope_ref[0, 0:1, 0:1])

    kt_diag = q0 // tk

    def left_body(kt, carry):
        tile(kt, w_left, None)
        return carry

    def right_body(kt, carry):
        tile(kt, w_right, None)
        return carry

    lax.fori_loop(0, kt_diag, left_body, 0)
    tile(kt_diag, w_left, diag_fix)
    lax.fori_loop(kt_diag + 1, nk, right_body, 0)

    lam = (jnp.exp(jnp.sum(lq1_ref[...] * lk1_ref[...], axis=-1, keepdims=True))
           - jnp.exp(jnp.sum(lq2_ref[...] * lk2_ref[...], axis=-1, keepdims=True)) + lam_init)
    o_t = acc_ref[0] / l_ref[0] - lam * (acc_ref[1] / l_ref[1])
    o = o_t.T
    o_ref[...] = (_rms(o, g_ref[...], SUBLN_EPS) * (1.0 - lam_init)).astype(BF16)


def _alibi_tables(s_len):
    pos = jnp.arange(s_len)
    hi = ((pos // POS_SPLIT) * POS_SPLIT).astype(F32)
    lo = (pos % POS_SPLIT).astype(F32)
    one = jnp.ones_like(hi)
    slopes = jnp.asarray([2.0 ** (-8.0 * (h + 1) / A_HEADS) for h in range(A_HEADS)], F32)
    pad = jnp.zeros((A_HEADS, s_len, LANES - 4), F32)

    def per_head(cols):
        return jnp.concatenate([jnp.stack(cols, axis=-1), pad], axis=-1).astype(BF16)

    sl = slopes[:, None]
    ones = jnp.broadcast_to(one[None], (A_HEADS, s_len))
    kaug = per_head([sl * hi[None], sl * lo[None], ones, ones])
    qaug = per_head([ones, ones, -sl * hi[None], -sl * lo[None]])
    slope_tab = jnp.broadcast_to(slopes[:, None, None], (A_HEADS, 8, LANES))
    return qaug, kaug, slope_tab


def _attention_a(qa, ka, vat, bsz, s_len, lq1, lk1, lq2, lk2, g_subln, lam_init):
    nq = s_len // TQ_A
    qaug, kaug, slope_tab = _alibi_tables(s_len)
    vec = lambda a: a.astype(F32)[None]
    small = lambda b, h, i: (0, 0)
    return pl.pallas_call(
        functools.partial(_attn_a_kernel, lam_init=lam_init),
        grid=(bsz, A_HEADS, nq),
        in_specs=[
            pl.BlockSpec((1, HEAD_DIM), small),
            pl.BlockSpec((1, HEAD_DIM), small),
            pl.BlockSpec((1, HEAD_DIM), small),
            pl.BlockSpec((1, HEAD_DIM), small),
            pl.BlockSpec((1, 8, LANES), lambda b, h, i: (h, 0, 0)),
            pl.BlockSpec((1, A_VDIM), small),
            pl.BlockSpec((TQ_A, A_VDIM), lambda b, h, i: (b * nq + i, h)),
            pl.BlockSpec((1, TQ_A, LANES), lambda b, h, i: (h, i, 0)),
            pl.BlockSpec((s_len, A_VDIM), lambda b, h, i: (b, h)),
            pl.BlockSpec((1, s_len, LANES), lambda b, h, i: (h, 0, 0)),
            pl.BlockSpec((1, s_len // TK_A, A_VDIM, TK_A), lambda b, h, i: (b, 0, h, 0)),
        ],
        out_specs=pl.BlockSpec((TQ_A, A_VDIM), lambda b, h, i: (b * nq + i, h)),
        out_shape=jax.ShapeDtypeStruct((bsz * s_len, A_WIDTH), BF16),
        scratch_shapes=[
            pltpu.VMEM((2, 1, TQ_A), F32),
            pltpu.VMEM((2, 1, TQ_A), F32),
            pltpu.VMEM((2, A_VDIM, TQ_A), F32),
        ],
        compiler_params=pltpu.CompilerParams(
            dimension_semantics=("parallel", "parallel", "arbitrary"), vmem_limit_bytes=VMEM_LIMIT),
        name="attn_a",
    )(vec(lq1), vec(lk1), vec(lq2), vec(lk2), slope_tab, vec(g_subln), qa, qaug, ka, kaug, vat)


def _attn_b_kernel(q_ref, k_ref, vt_ref, o_ref, m_ref, l_ref, acc_ref):
    tq = q_ref.shape[1]
    tk = vt_ref.shape[-1]
    nk = vt_ref.shape[1]
    q = q_ref[...].reshape(B_GROUP * tq, HEAD_DIM)

    m_ref[...] = jnp.full(m_ref.shape, NEG_BIG, F32)
    l_ref[...] = jnp.zeros(l_ref.shape, F32)
    acc_ref[...] = jnp.zeros(acc_ref.shape, F32)

    def body(kt, carry):
        k0 = pl.multiple_of(kt * tk, tk)
        s = lax.dot_general(k_ref[0, pl.ds(k0, tk), :], q, NT_DIMS, preferred_element_type=F32)
        _softmax_step(s, vt_ref[0, kt], m_ref, l_ref, acc_ref, 0)
        return carry

    lax.fori_loop(0, nk, body, 0)

    o_t = acc_ref[0] / l_ref[0]
    stacked = jnp.concatenate([o_t[:, h * tq:(h + 1) * tq] for h in range(B_GROUP)], axis=0)
    o_ref[...] = stacked.T.astype(BF16)


def _attention_b(qb, kb, vbt, bsz, s_len):
    nq = s_len // TQ_B
    return pl.pallas_call(
        _attn_b_kernel,
        grid=(bsz, B_KV_HEADS, nq),
        in_specs=[
            pl.BlockSpec((B_GROUP, TQ_B, HEAD_DIM), lambda b, g, i: (g, b * nq + i, 0)),
            pl.BlockSpec((1, s_len, HEAD_DIM), lambda b, g, i: (g, b, 0)),
            pl.BlockSpec((1, s_len // TK_B, HEAD_DIM, TK_B), lambda b, g, i: (b, 0, g, 0)),
        ],
        out_specs=pl.BlockSpec((TQ_B, B_GROUP * HEAD_DIM), lambda b, g, i: (b * nq + i, g)),
        out_shape=jax.ShapeDtypeStruct((bsz * s_len, B_WIDTH), BF16),
        scratch_shapes=[
            pltpu.VMEM((1, 1, B_GROUP * TQ_B), F32),
            pltpu.VMEM((1, 1, B_GROUP * TQ_B), F32),
            pltpu.VMEM((1, HEAD_DIM, B_GROUP * TQ_B), F32),
        ],
        compiler_params=pltpu.CompilerParams(
            dimension_semantics=("parallel", "parallel", "arbitrary"), vmem_limit_bytes=VMEM_LIMIT),
        name="attn_b",
    )(qb, kb, vbt)


def _post_kernel(x_ref, oa_ref, ob_ref, p_ref, wout_ref, gmlp_ref, w1_ref, w2_ref, gple_ref,
                 wg_ref, wp_ref, gfin_ref, y_ref, *, final_norm):
    mixed = jnp.concatenate([oa_ref[...], ob_ref[...]], axis=1)
    h = x_ref[...] + jnp.dot(mixed, wout_ref[...], preferred_element_type=F32)

    n2 = _rms(h, gmlp_ref[...], NORM_EPS).astype(BF16)
    for c in range(D_FF // FF_CHUNK):
        a = jnp.maximum(jnp.dot(n2, w1_ref[:, c * FF_CHUNK:(c + 1) * FF_CHUNK],
                                preferred_element_type=F32), 0.0)
        h = h + jnp.dot((a * a).astype(BF16), w2_ref[c * FF_CHUNK:(c + 1) * FF_CHUNK, :],
                        preferred_element_type=F32)

    n3 = _rms(h, gple_ref[...], NORM_EPS).astype(BF16)
    gate = jax.nn.sigmoid(jnp.dot(n3, wg_ref[...], preferred_element_type=F32))
    h = h + gate * jnp.dot(p_ref[...].astype(BF16), wp_ref[...], preferred_element_type=F32)
    if final_norm:
        h = _rms(h, gfin_ref[...], NORM_EPS)
    y_ref[...] = h


def _post(x2d, oa, ob, p2d, w_out, g_mlp, w_ff1, w_ff2, g_ple, w_gate, w_proj, g_final, final_norm):
    t_total = x2d.shape[0]
    tm = TM_POST
    const = lambda t: (0, 0)
    resident = functools.partial(pl.BlockSpec, index_map=const, pipeline_mode=pl.Buffered(1))
    row = lambda a: a.astype(F32)[None]
    return pl.pallas_call(
        functools.partial(_post_kernel, final_norm=final_norm),
        grid=(t_total // tm,),
        in_specs=[
            pl.BlockSpec((tm, D_MODEL), lambda t: (t, 0)),
            pl.BlockSpec((tm, A_WIDTH), lambda t: (t, 0)),
            pl.BlockSpec((tm, B_WIDTH), lambda t: (t, 0)),
            pl.BlockSpec((tm, PLE_DIM), lambda t: (t, 0)),
            resident((D_MODEL, D_MODEL)),
            pl.BlockSpec((1, D_MODEL), const),
            resident((D_MODEL, D_FF)),
            resident((D_FF, D_MODEL)),
            pl.BlockSpec((1, D_MODEL), const),
            resident((D_MODEL, D_MODEL)),
            resident((PLE_DIM, D_MODEL)),
            pl.BlockSpec((1, D_MODEL), const),
        ],
        out_specs=pl.BlockSpec((tm, D_MODEL), lambda t: (t, 0)),
        out_shape=jax.ShapeDtypeStruct((t_total, D_MODEL), F32),
        compiler_params=pltpu.CompilerParams(
            dimension_semantics=("parallel",), vmem_limit_bytes=VMEM_LIMIT),
        name="post",
    )(x2d, oa, ob, p2d, w_out.astype(BF16), row(g_mlp), w_ff1.astype(BF16), w_ff2.astype(BF16),
      row(g_ple), w_gate.astype(BF16), w_proj.astype(BF16), row(g_final))


def _trunk(x, p, w_in, g_mix, lambda_q1, lambda_k1, lambda_q2, lambda_k2, g_subln, g_qnorm, g_knorm,
           w_out, g_mlp, w_ff1, w_ff2, g_ple, w_ple_gate, w_ple_proj, g_final):
    bsz, s_len, _ = x.shape
    depth = w_in.shape[0]
    h2d = x.reshape(bsz * s_len, D_MODEL)
    for l in range(depth):
        qa, ka, vat, qb, kb, vbt = _project(h2d, bsz, s_len, g_mix[l], w_in[l], g_qnorm[l], g_knorm[l])
        oa = _attention_a(qa, ka, vat, bsz, s_len, lambda_q1[l], lambda_k1[l], lambda_q2[l],
                          lambda_k2[l], g_subln[l], _lambda_init(l))
        ob = _attention_b(qb, kb, vbt, bsz, s_len)
        h2d = _post(h2d, oa, ob, p[l].reshape(bsz * s_len, PLE_DIM), w_out[l], g_mlp[l], w_ff1[l],
                    w_ff2[l], g_ple[l], w_ple_gate[l], w_ple_proj[l], g_final,
                    final_norm=(l == depth - 1))
    return h2d.reshape(bsz, s_len, D_MODEL)


def kernel(x_prompt, x_sample, p_prompt, p_sample, w_in, g_mix, lambda_q1, lambda_k1, lambda_q2, lambda_k2,
           g_subln, g_qnorm, g_knorm, w_out, g_mlp, w_ff1, w_ff2, g_ple, w_ple_gate, w_ple_proj, g_final):
    weights = (w_in, g_mix, lambda_q1, lambda_k1, lambda_q2, lambda_k2, g_subln, g_qnorm, g_knorm,
               w_out, g_mlp, w_ff1, w_ff2, g_ple, w_ple_gate, w_ple_proj, g_final)
    return (_trunk(x_prompt, p_prompt, *weights), _trunk(x_sample, p_sample, *weights))
```

```python
import functools
import math

import jax
import jax.numpy as jnp
from jax import lax
from jax.experimental import pallas as pl
from jax.experimental.pallas import tpu as pltpu

D_MODEL = 1024
HEAD_DIM = 64
A_HEADS = 4
A_VDIM = 2 * HEAD_DIM
A_WIDTH = A_HEADS * A_VDIM
B_HEADS = 8
B_KV_HEADS = 2
B_GROUP = B_HEADS // B_KV_HEADS
B_WIDTH = B_HEADS * HEAD_DIM
B_KV_WIDTH = B_KV_HEADS * HEAD_DIM
IN_COLS = 3 * A_WIDTH + B_WIDTH + 2 * B_KV_WIDTH
D_FF = 4 * D_MODEL
PLE_DIM = 256
GRID_W = 64
ROPE_THETA = 10000.0
ROPE_HALF = HEAD_DIM // 2
ROPE_QUARTER = ROPE_HALF // 2
NORM_EPS = 1e-6
SUBLN_EPS = 1e-5
SCALE = HEAD_DIM ** -0.5

LANES = 128
POS_SPLIT = 128
NEG_BIG = -1e30

TM_PROJ = 512
TM_POST = 256
TQ_A = 256
TK_A = 512
TQ_B = 256
TK_B = 256
FF_CHUNK = 1024
VMEM_LIMIT = 56 * 1024 * 1024

BF16 = jnp.bfloat16
F32 = jnp.float32


def _lambda_init(layer_idx):
    return 0.8 - 0.6 * math.exp(-0.3 * layer_idx)


def _rms(x, g, eps):
    return x * lax.rsqrt(jnp.mean(x * x, axis=-1, keepdims=True) + eps) * g


def _group_sumsq(y, ones):
    y2 = y * y
    hi = y2.astype(BF16)
    lo = (y2 - hi.astype(F32)).astype(BF16)
    return jnp.dot(jnp.concatenate([hi, lo], axis=1), jnp.concatenate([ones, ones], axis=0),
                   preferred_element_type=F32)


def _rope_chunk(y, cos, sin_lo, sin_hi):
    return (y * cos
            + pltpu.roll(y, LANES - ROPE_QUARTER, 1) * sin_lo
            + pltpu.roll(y, ROPE_QUARTER, 1) * sin_hi)


def _proj_kernel(x_ref, gmix_ref, win_ref, ones_ref, gq_ref, gk_ref, cos_ref, slo_ref, shi_ref,
                 qat_ref, ka_ref, vat_ref, qbt_ref, kb_ref, vbt_ref, *, tka, tkb):
    x = x_ref[...]
    n = _rms(x, gmix_ref[...], NORM_EPS)
    z = jnp.dot(n.astype(BF16), win_ref[...], preferred_element_type=F32)
    tm = x.shape[0]

    c0 = 0
    qat_ref[0] = (z[:, c0:c0 + A_WIDTH] * SCALE).T.astype(BF16)
    c0 += A_WIDTH
    ka_ref[...] = z[:, c0:c0 + A_WIDTH].astype(BF16)
    c0 += A_WIDTH
    va = z[:, c0:c0 + A_WIDTH]
    c0 += A_WIDTH
    qb = z[:, c0:c0 + B_WIDTH]
    c0 += B_WIDTH
    kb = z[:, c0:c0 + B_KV_WIDTH]
    c0 += B_KV_WIDTH
    vb = z[:, c0:c0 + B_KV_WIDTH]

    for j in range(tm // tka):
        vat_ref[0, j] = va[j * tka:(j + 1) * tka, :].T.astype(BF16)
    for j in range(tm // tkb):
        vbt_ref[0, j] = vb[j * tkb:(j + 1) * tkb, :].T.astype(BF16)

    ones = ones_ref[...]
    cos = cos_ref[...]
    slo = slo_ref[...]
    shi = shi_ref[...]

    qn = qb * lax.rsqrt(_group_sumsq(qb, ones) * (1.0 / HEAD_DIM) + NORM_EPS) * gq_ref[...]
    for j in range(B_WIDTH // LANES):
        rot = _rope_chunk(qn[:, j * LANES:(j + 1) * LANES], cos, slo, shi) * SCALE
        qbt_ref[0, j * LANES:(j + 1) * LANES, :] = rot.T.astype(BF16)

    kn = kb * lax.rsqrt(_group_sumsq(kb, ones[:B_KV_WIDTH, :B_KV_WIDTH]) * (1.0 / HEAD_DIM)
                        + NORM_EPS) * gk_ref[...]
    rot = _rope_chunk(kn, cos, slo, shi)
    kb_ref[0] = rot[:, :HEAD_DIM].astype(BF16)
    kb_ref[1] = rot[:, HEAD_DIM:].astype(BF16)


def _rope_tables(s_len):
    t = jnp.arange(s_len)
    inv_freq = ROPE_THETA ** (-jnp.arange(0, ROPE_HALF, 2, dtype=F32) / ROPE_HALF)
    ang_r = (t // GRID_W).astype(F32)[:, None] * inv_freq[None]
    ang_c = (t % GRID_W).astype(F32)[:, None] * inv_freq[None]
    zero = jnp.zeros_like(ang_r)
    cos = jnp.concatenate([jnp.cos(ang_r)] * 2 + [jnp.cos(ang_c)] * 2, axis=-1)
    sin_lo = jnp.concatenate([-jnp.sin(ang_r), zero, -jnp.sin(ang_c), zero], axis=-1)
    sin_hi = jnp.concatenate([zero, jnp.sin(ang_r), zero, jnp.sin(ang_c)], axis=-1)
    rep = LANES // HEAD_DIM
    return tuple(jnp.tile(a, (1, rep)) for a in (cos, sin_lo, sin_hi))


def _project(x2d, bsz, s_len, g_mix, w_in, g_qnorm, g_knorm):
    t_total = x2d.shape[0]
    tm = TM_PROJ
    tiles_per_b = s_len // tm
    cos, slo, shi = _rope_tables(s_len)
    grp = jnp.arange(B_WIDTH) // HEAD_DIM
    ones = (grp[:, None] == grp[None, :]).astype(BF16)
    gq = jnp.tile(g_qnorm.astype(F32), B_HEADS)[None]
    gk = jnp.tile(g_knorm.astype(F32), B_KV_HEADS)[None]

    const = lambda t: (0, 0)
    per_batch_t = lambda t: (t // tiles_per_b, 0, t % tiles_per_b)
    per_batch_tiles = lambda t: (t // tiles_per_b, t % tiles_per_b, 0, 0)
    out_shape = (
        jax.ShapeDtypeStruct((bsz, A_WIDTH, s_len), BF16),
        jax.ShapeDtypeStruct((t_total, A_WIDTH), BF16),
        jax.ShapeDtypeStruct((bsz, s_len // TK_A, A_WIDTH, TK_A), BF16),
        jax.ShapeDtypeStruct((bsz, B_WIDTH, s_len), BF16),
        jax.ShapeDtypeStruct((B_KV_HEADS, t_total, HEAD_DIM), BF16),
        jax.ShapeDtypeStruct((bsz, s_len // TK_B, B_KV_WIDTH, TK_B), BF16),
    )
    return pl.pallas_call(
        functools.partial(_proj_kernel, tka=TK_A, tkb=TK_B),
        grid=(t_total // tm,),
        in_specs=[
            pl.BlockSpec((tm, D_MODEL), lambda t: (t, 0)),
            pl.BlockSpec((1, D_MODEL), const),
            pl.BlockSpec((D_MODEL, IN_COLS), const),
            pl.BlockSpec((B_WIDTH, B_WIDTH), const),
            pl.BlockSpec((1, B_WIDTH), const),
            pl.BlockSpec((1, B_KV_WIDTH), const),
            pl.BlockSpec((tm, LANES), lambda t: (t % tiles_per_b, 0)),
            pl.BlockSpec((tm, LANES), lambda t: (t % tiles_per_b, 0)),
            pl.BlockSpec((tm, LANES), lambda t: (t % tiles_per_b, 0)),
        ],
        out_specs=(
            pl.BlockSpec((1, A_WIDTH, tm), per_batch_t),
            pl.BlockSpec((tm, A_WIDTH), lambda t: (t, 0)),
            pl.BlockSpec((1, tm // TK_A, A_WIDTH, TK_A), per_batch_tiles),
            pl.BlockSpec((1, B_WIDTH, tm), per_batch_t),
            pl.BlockSpec((B_KV_HEADS, tm, HEAD_DIM), lambda t: (0, t, 0)),
            pl.BlockSpec((1, tm // TK_B, B_KV_WIDTH, TK_B), per_batch_tiles),
        ),
        out_shape=out_shape,
        compiler_params=pltpu.CompilerParams(
            dimension_semantics=("parallel",), vmem_limit_bytes=VMEM_LIMIT),
        name="proj",
    )(x2d, g_mix[None].astype(F32), w_in.astype(BF16), ones, gq, gk, cos, slo, shi)


def _col_max(m_run, s):
    return jnp.maximum(m_run, jnp.max(s, axis=0, keepdims=True))


def _accumulate(s, vt, m_prev, m_cur, l_ref, acc_ref, idx):
    alpha = jnp.exp(m_prev - m_cur)
    p = jnp.exp(s - m_cur)
    l_ref[idx] = alpha * l_ref[idx] + jnp.sum(p, axis=0, keepdims=True)
    acc_ref[idx] = alpha * acc_ref[idx] + jnp.dot(vt, p.astype(BF16), preferred_element_type=F32)


def _pipelined_sweep(n_tiles, score_stage, value_stage, s_bufs, m_init):
    s_even, s_odd = s_bufs
    m0 = score_stage(0, m_init, s_even, first=True)

    def pair(jj, carry):
        m_prev, m_cur = carry
        i = 2 * jj
        m_nxt = score_stage(i + 1, m_cur, s_odd)
        value_stage(i, s_even, m_prev, m_cur)
        m_nxt2 = score_stage(i + 2, m_nxt, s_even)
        value_stage(i + 1, s_odd, m_cur, m_nxt)
        return m_nxt, m_nxt2

    m_prev, m_cur = lax.fori_loop(0, (n_tiles - 2) // 2, pair, (m0, m0))
    m_last = score_stage(n_tiles - 1, m_cur, s_odd)
    value_stage(n_tiles - 2, s_even, m_prev, m_cur)
    value_stage(n_tiles - 1, s_odd, m_cur, m_last)


def _attn_a_kernel(lq1_ref, lk1_ref, lq2_ref, lk2_ref, slope_ref, g_ref, qt_ref, qaugt_ref, k_ref,
                   kaug_ref, vt_ref, o_ref, w_ref, s0_ref, s1_ref, l_ref, acc_ref, *, lam_init):
    tq = qt_ref.shape[-1]
    tk = vt_ref.shape[-1]
    nk = vt_ref.shape[1]
    q0 = pl.program_id(2) * tq
    kt_diag = q0 // tk

    qt = qt_ref[0]
    qaugt = qaugt_ref[0]
    zero = jnp.zeros((HEAD_DIM, tq), BF16)
    for c in range(2):
        rows = [qt[:HEAD_DIM], zero] if c == 0 else [zero, qt[HEAD_DIM:]]
        w_ref[0, c] = jnp.concatenate(rows + [qaugt], axis=0)
        w_ref[1, c] = jnp.concatenate(rows + [-qaugt], axis=0)

    l_ref[...] = jnp.zeros(l_ref.shape, F32)
    acc_ref[...] = jnp.zeros(acc_ref.shape, F32)

    def tile_of(j):
        left = j - 1 < kt_diag
        kt = jnp.where(j == 0, kt_diag, jnp.where(left, j - 1, j))
        return kt, jnp.where(left, 0, 1)

    def score_stage(j, m_run, s_buf, first=False):
        kt, side = (kt_diag, 0) if first else tile_of(j)
        k0 = pl.multiple_of(kt * tk, tk)
        kcat = jnp.concatenate([k_ref[pl.ds(k0, tk), :], kaug_ref[0, pl.ds(k0, tk), :]], axis=1)
        if first:
            kpos = k0 + lax.broadcasted_iota(jnp.int32, (tk, tq), 0)
            qpos = q0 + lax.broadcasted_iota(jnp.int32, (tk, tq), 1)
            fix = jnp.maximum(kpos - qpos, 0).astype(F32) * (-2.0 * slope_ref[0, 0:1, 0:1])
        out = []
        for c in range(2):
            s = jnp.dot(kcat, w_ref[side, c], preferred_element_type=F32)
            if first:
                s = s + fix
            s_buf[c] = s
            out.append(_col_max(m_run[c], s))
        return tuple(out)

    def value_stage(j, s_buf, m_prev, m_cur):
        vt = vt_ref[0, tile_of(j)[0]]
        for c in range(2):
            _accumulate(s_buf[c], vt, m_prev[c], m_cur[c], l_ref, acc_ref, c)

    m_init = (jnp.full((1, tq), NEG_BIG, F32),) * 2
    _pipelined_sweep(nk, score_stage, value_stage, (s0_ref, s1_ref), m_init)

    lam = (jnp.exp(jnp.sum(lq1_ref[...] * lk1_ref[...], axis=-1, keepdims=True))
           - jnp.exp(jnp.sum(lq2_ref[...] * lk2_ref[...], axis=-1, keepdims=True)) + lam_init)
    o_t = acc_ref[0] / l_ref[0] - lam * (acc_ref[1] / l_ref[1])
    o = o_t.T
    o_ref[...] = (_rms(o, g_ref[...], SUBLN_EPS) * (1.0 - lam_init)).astype(BF16)


def _alibi_tables(s_len):
    pos = jnp.arange(s_len)
    hi = ((pos // POS_SPLIT) * POS_SPLIT).astype(F32)
    lo = (pos % POS_SPLIT).astype(F32)
    one = jnp.ones_like(hi)
    slopes = jnp.asarray([2.0 ** (-8.0 * (h + 1) / A_HEADS) for h in range(A_HEADS)], F32)
    pad = jnp.zeros((A_HEADS, s_len, LANES - 4), F32)

    def per_head(cols):
        return jnp.concatenate([jnp.stack(cols, axis=-1), pad], axis=-1).astype(BF16)

    sl = slopes[:, None]
    ones = jnp.broadcast_to(one[None], (A_HEADS, s_len))
    kaug = per_head([sl * hi[None], sl * lo[None], ones, ones])
    qaug = per_head([ones, ones, -sl * hi[None], -sl * lo[None]])
    slope_tab = jnp.broadcast_to(slopes[:, None, None], (A_HEADS, 8, LANES))
    return jnp.swapaxes(qaug, 1, 2), kaug, slope_tab


def _attention_a(qat, ka, vat, bsz, s_len, lq1, lk1, lq2, lk2, g_subln, lam_init):
    nq = s_len // TQ_A
    qaugt, kaug, slope_tab = _alibi_tables(s_len)
    vec = lambda a: a.astype(F32)[None]
    small = lambda b, h, i: (0, 0)
    return pl.pallas_call(
        functools.partial(_attn_a_kernel, lam_init=lam_init),
        grid=(bsz, A_HEADS, nq),
        in_specs=[
            pl.BlockSpec((1, HEAD_DIM), small),
            pl.BlockSpec((1, HEAD_DIM), small),
            pl.BlockSpec((1, HEAD_DIM), small),
            pl.BlockSpec((1, HEAD_DIM), small),
            pl.BlockSpec((1, 8, LANES), lambda b, h, i: (h, 0, 0)),
            pl.BlockSpec((1, A_VDIM), small),
            pl.BlockSpec((1, A_VDIM, TQ_A), lambda b, h, i: (b, h, i)),
            pl.BlockSpec((1, LANES, TQ_A), lambda b, h, i: (h, 0, i)),
            pl.BlockSpec((s_len, A_VDIM), lambda b, h, i: (b, h)),
            pl.BlockSpec((1, s_len, LANES), lambda b, h, i: (h, 0, 0)),
            pl.BlockSpec((1, s_len // TK_A, A_VDIM, TK_A), lambda b, h, i: (b, 0, h, 0)),
        ],
        out_specs=pl.BlockSpec((TQ_A, A_VDIM), lambda b, h, i: (b * nq + i, h)),
        out_shape=jax.ShapeDtypeStruct((bsz * s_len, A_WIDTH), BF16),
        scratch_shapes=[
            pltpu.VMEM((2, 2, A_VDIM + LANES, TQ_A), BF16),
            pltpu.VMEM((2, TK_A, TQ_A), F32),
            pltpu.VMEM((2, TK_A, TQ_A), F32),
            pltpu.VMEM((2, 1, TQ_A), F32),
            pltpu.VMEM((2, A_VDIM, TQ_A), F32),
        ],
        compiler_params=pltpu.CompilerParams(
            dimension_semantics=("parallel", "parallel", "arbitrary"), vmem_limit_bytes=VMEM_LIMIT),
        name="attn_a",
    )(vec(lq1), vec(lk1), vec(lq2), vec(lk2), slope_tab, vec(g_subln), qat, qaugt, ka, kaug, vat)


def _attn_b_kernel(qt_ref, k_ref, vt_ref, o_ref, w_ref, s0_ref, s1_ref, l_ref, acc_ref):
    tq = qt_ref.shape[-1]
    tk = vt_ref.shape[-1]
    nk = vt_ref.shape[1]

    for h in range(B_GROUP):
        w_ref[:, h * tq:(h + 1) * tq] = qt_ref[0, h * HEAD_DIM:(h + 1) * HEAD_DIM, :]

    l_ref[...] = jnp.zeros(l_ref.shape, F32)
    acc_ref[...] = jnp.zeros(acc_ref.shape, F32)

    def score_stage(j, m_run, s_buf, first=False):
        k0 = pl.multiple_of(j * tk, tk)
        s = jnp.dot(k_ref[0, pl.ds(k0, tk), :], w_ref[...], preferred_element_type=F32)
        s_buf[...] = s
        return _col_max(m_run, s)

    def value_stage(j, s_buf, m_prev, m_cur):
        _accumulate(s_buf[...], vt_ref[0, j], m_prev, m_cur, l_ref, acc_ref, 0)

    m_init = jnp.full((1, B_GROUP * tq), NEG_BIG, F32)
    _pipelined_sweep(nk, score_stage, value_stage, (s0_ref, s1_ref), m_init)

    o_t = acc_ref[0] / l_ref[0]
    stacked = jnp.concatenate([o_t[:, h * tq:(h + 1) * tq] for h in range(B_GROUP)], axis=0)
    o_ref[...] = stacked.T.astype(BF16)


def _attention_b(qbt, kb, vbt, bsz, s_len):
    nq = s_len // TQ_B
    ncol = B_GROUP * TQ_B
    return pl.pallas_call(
        _attn_b_kernel,
        grid=(bsz, B_KV_HEADS, nq),
        in_specs=[
            pl.BlockSpec((1, B_GROUP * HEAD_DIM, TQ_B), lambda b, g, i: (b, g, i)),
            pl.BlockSpec((1, s_len, HEAD_DIM), lambda b, g, i: (g, b, 0)),
            pl.BlockSpec((1, s_len // TK_B, HEAD_DIM, TK_B), lambda b, g, i: (b, 0, g, 0)),
        ],
        out_specs=pl.BlockSpec((TQ_B, B_GROUP * HEAD_DIM), lambda b, g, i: (b * nq + i, g)),
        out_shape=jax.ShapeDtypeStruct((bsz * s_len, B_WIDTH), BF16),
        scratch_shapes=[
            pltpu.VMEM((HEAD_DIM, ncol), BF16),
            pltpu.VMEM((TK_B, ncol), F32),
            pltpu.VMEM((TK_B, ncol), F32),
            pltpu.VMEM((1, 1, ncol), F32),
            pltpu.VMEM((1, HEAD_DIM, ncol), F32),
        ],
        compiler_params=pltpu.CompilerParams(
            dimension_semantics=("parallel", "parallel", "arbitrary"), vmem_limit_bytes=VMEM_LIMIT),
        name="attn_b",
    )(qbt, kb, vbt)


def _post_kernel(x_ref, oa_ref, ob_ref, p_ref, wout_ref, gmlp_ref, w1_ref, w2_ref, gple_ref,
                 wg_ref, wp_ref, gfin_ref, y_ref, *, final_norm):
    mixed = jnp.concatenate([oa_ref[...], ob_ref[...]], axis=1)
    h = x_ref[...] + jnp.dot(mixed, wout_ref[...], preferred_element_type=F32)

    n2 = _rms(h, gmlp_ref[...], NORM_EPS).astype(BF16)
    for c in range(D_FF // FF_CHUNK):
        a = jnp.maximum(jnp.dot(n2, w1_ref[:, c * FF_CHUNK:(c + 1) * FF_CHUNK],
                                preferred_element_type=F32), 0.0)
        h = h + jnp.dot((a * a).astype(BF16), w2_ref[c * FF_CHUNK:(c + 1) * FF_CHUNK, :],
                        preferred_element_type=F32)

    n3 = _rms(h, gple_ref[...], NORM_EPS).astype(BF16)
    gate = jax.nn.sigmoid(jnp.dot(n3, wg_ref[...], preferred_element_type=F32))
    h = h + gate * jnp.dot(p_ref[...].astype(BF16), wp_ref[...], preferred_element_type=F32)
    if final_norm:
        h = _rms(h, gfin_ref[...], NORM_EPS)
    y_ref[...] = h


def _post(x2d, oa, ob, p2d, w_out, g_mlp, w_ff1, w_ff2, g_ple, w_gate, w_proj, g_final, final_norm):
    t_total = x2d.shape[0]
    tm = TM_POST
    const = lambda t: (0, 0)
    resident = functools.partial(pl.BlockSpec, index_map=const, pipeline_mode=pl.Buffered(1))
    row = lambda a: a.astype(F32)[None]
    return pl.pallas_call(
        functools.partial(_post_kernel, final_norm=final_norm),
        grid=(t_total // tm,),
        in_specs=[
            pl.BlockSpec((tm, D_MODEL), lambda t: (t, 0)),
            pl.BlockSpec((tm, A_WIDTH), lambda t: (t, 0)),
            pl.BlockSpec((tm, B_WIDTH), lambda t: (t, 0)),
            pl.BlockSpec((tm, PLE_DIM), lambda t: (t, 0)),
            resident((D_MODEL, D_MODEL)),
            pl.BlockSpec((1, D_MODEL), const),
            resident((D_MODEL, D_FF)),
            resident((D_FF, D_MODEL)),
            pl.BlockSpec((1, D_MODEL), const),
            resident((D_MODEL, D_MODEL)),
            resident((PLE_DIM, D_MODEL)),
            pl.BlockSpec((1, D_MODEL), const),
        ],
        out_specs=pl.BlockSpec((tm, D_MODEL), lambda t: (t, 0)),
        out_shape=jax.ShapeDtypeStruct((t_total, D_MODEL), F32),
        compiler_params=pltpu.CompilerParams(
            dimension_semantics=("parallel",), vmem_limit_bytes=VMEM_LIMIT),
        name="post",
    )(x2d, oa, ob, p2d, w_out.astype(BF16), row(g_mlp), w_ff1.astype(BF16), w_ff2.astype(BF16),
      row(g_ple), w_gate.astype(BF16), w_proj.astype(BF16), row(g_final))


def _trunk(x, p, w_in, g_mix, lambda_q1, lambda_k1, lambda_q2, lambda_k2, g_subln, g_qnorm, g_knorm,
           w_out, g_mlp, w_ff1, w_ff2, g_ple, w_ple_gate, w_ple_proj, g_final):
    bsz, s_len, _ = x.shape
    depth = w_in.shape[0]
    h2d = x.reshape(bsz * s_len, D_MODEL)
    for l in range(depth):
        qat, ka, vat, qbt, kb, vbt = _project(h2d, bsz, s_len, g_mix[l], w_in[l], g_qnorm[l], g_knorm[l])
        oa = _attention_a(qat, ka, vat, bsz, s_len, lambda_q1[l], lambda_k1[l], lambda_q2[l],
                          lambda_k2[l], g_subln[l], _lambda_init(l))
        ob = _attention_b(qbt, kb, vbt, bsz, s_len)
        h2d = _post(h2d, oa, ob, p[l].reshape(bsz * s_len, PLE_DIM), w_out[l], g_mlp[l], w_ff1[l],
                    w_ff2[l], g_ple[l], w_ple_gate[l], w_ple_proj[l], g_final,
                    final_norm=(l == depth - 1))
    return h2d.reshape(bsz, s_len, D_MODEL)


def kernel(x_prompt, x_sample, p_prompt, p_sample, w_in, g_mix, lambda_q1, lambda_k1, lambda_q2, lambda_k2,
           g_subln, g_qnorm, g_knorm, w_out, g_mlp, w_ff1, w_ff2, g_ple, w_ple_gate, w_ple_proj, g_final):
    weights = (w_in, g_mix, lambda_q1, lambda_k1, lambda_q2, lambda_k2, g_subln, g_qnorm, g_knorm,
               w_out, g_mlp, w_ff1, w_ff2, g_ple, w_ple_gate, w_ple_proj, g_final)
    return (_trunk(x_prompt, p_prompt, *weights), _trunk(x_sample, p_sample, *weights))
```

```python
import functools
import math

import jax
import jax.numpy as jnp
import ml_dtypes
import numpy as np
from jax import lax
from jax.experimental import pallas as pl
from jax.experimental.pallas import tpu as pltpu

D_MODEL = 1024
HEAD_DIM = 64
A_HEADS = 4
A_VDIM = 2 * HEAD_DIM
A_WIDTH = A_HEADS * A_VDIM
B_HEADS = 8
B_KV_HEADS = 2
B_GROUP = B_HEADS // B_KV_HEADS
B_WIDTH = B_HEADS * HEAD_DIM
B_KV_WIDTH = B_KV_HEADS * HEAD_DIM
IN_COLS = 3 * A_WIDTH + B_WIDTH + 2 * B_KV_WIDTH
D_FF = 4 * D_MODEL
PLE_DIM = 256
GRID_W = 64
ROPE_THETA = 10000.0
ROPE_HALF = HEAD_DIM // 2
ROPE_QUARTER = ROPE_HALF // 2
NORM_EPS = 1e-6
SUBLN_EPS = 1e-5
SCALE = HEAD_DIM ** -0.5


def _bf16_pieces(x, n):
    out = []
    for _ in range(n):
        piece = float(np.float32(x).astype(ml_dtypes.bfloat16))
        out.append(piece)
        x -= piece
    return out


LOG2E_PIECES = _bf16_pieces(math.log2(math.e), 3)
LOG2E = float(np.float32(sum(LOG2E_PIECES)))
Q_SCALE = SCALE * LOG2E
ONES_ROWS = 16

LANES = 128
POS_SPLIT = 128
NEG_BIG = -1e30

TM_PROJ = 512
TM_POST = 256
TQ_A = 256
TK_A = 512
TQ_B = 256
TK_B = 256
SWEEP_UNROLL_A = 6
SWEEP_UNROLL_B = 8
FF_CHUNK = 1024
VMEM_LIMIT = 56 * 1024 * 1024

BF16 = jnp.bfloat16
F32 = jnp.float32


def _lambda_init(layer_idx):
    return 0.8 - 0.6 * math.exp(-0.3 * layer_idx)


def _rms(x, g, eps):
    return x * lax.rsqrt(jnp.mean(x * x, axis=-1, keepdims=True) + eps) * g


def _group_sumsq(y, ones):
    y2 = y * y
    hi = y2.astype(BF16)
    lo = (y2 - hi.astype(F32)).astype(BF16)
    return jnp.dot(jnp.concatenate([hi, lo], axis=1), jnp.concatenate([ones, ones], axis=0),
                   preferred_element_type=F32)


def _rope_chunk(y, cos, sin_lo, sin_hi):
    return (y * cos
            + pltpu.roll(y, LANES - ROPE_QUARTER, 1) * sin_lo
            + pltpu.roll(y, ROPE_QUARTER, 1) * sin_hi)


def _store_values_t(vt_ref, v, tk, dv):
    tm, width = v.shape
    blk = dv + ONES_ROWS
    ones = jnp.ones((ONES_ROWS, tk), BF16)
    for j in range(tm // tk):
        vt = v[j * tk:(j + 1) * tk, :].T.astype(BF16)
        for h in range(width // dv):
            vt_ref[0, j, h * blk:h * blk + dv, :] = vt[h * dv:(h + 1) * dv]
            vt_ref[0, j, h * blk + dv:(h + 1) * blk, :] = ones


def _proj_kernel(x_ref, gmix_ref, win_ref, ones_ref, gq_ref, gk_ref, cos_ref, slo_ref, shi_ref,
                 qat_ref, ka_ref, vat_ref, qbt_ref, kb_ref, vbt_ref, *, tka, tkb):
    x = x_ref[...]
    n = _rms(x, gmix_ref[...], NORM_EPS)
    z = jnp.dot(n.astype(BF16), win_ref[...], preferred_element_type=F32)
    tm = x.shape[0]

    c0 = 0
    qat_ref[0] = (z[:, c0:c0 + A_WIDTH] * Q_SCALE).T.astype(BF16)
    c0 += A_WIDTH
    ka_ref[...] = z[:, c0:c0 + A_WIDTH].astype(BF16)
    c0 += A_WIDTH
    va = z[:, c0:c0 + A_WIDTH]
    c0 += A_WIDTH
    qb = z[:, c0:c0 + B_WIDTH]
    c0 += B_WIDTH
    kb = z[:, c0:c0 + B_KV_WIDTH]
    c0 += B_KV_WIDTH
    vb = z[:, c0:c0 + B_KV_WIDTH]

    _store_values_t(vat_ref, va, tka, A_VDIM)
    _store_values_t(vbt_ref, vb, tkb, HEAD_DIM)

    ones = ones_ref[...]
    cos = cos_ref[...]
    slo = slo_ref[...]
    shi = shi_ref[...]

    qn = qb * lax.rsqrt(_group_sumsq(qb, ones) * (1.0 / HEAD_DIM) + NORM_EPS) * gq_ref[...]
    for j in range(B_WIDTH // LANES):
        rot = _rope_chunk(qn[:, j * LANES:(j + 1) * LANES], cos, slo, shi) * Q_SCALE
        qbt_ref[0, j * LANES:(j + 1) * LANES, :] = rot.T.astype(BF16)

    kn = kb * lax.rsqrt(_group_sumsq(kb, ones[:B_KV_WIDTH, :B_KV_WIDTH]) * (1.0 / HEAD_DIM)
                        + NORM_EPS) * gk_ref[...]
    rot = _rope_chunk(kn, cos, slo, shi)
    kb_ref[0] = rot[:, :HEAD_DIM].astype(BF16)
    kb_ref[1] = rot[:, HEAD_DIM:].astype(BF16)


def _rope_tables(s_len):
    t = jnp.arange(s_len)
    inv_freq = ROPE_THETA ** (-jnp.arange(0, ROPE_HALF, 2, dtype=F32) / ROPE_HALF)
    ang_r = (t // GRID_W).astype(F32)[:, None] * inv_freq[None]
    ang_c = (t % GRID_W).astype(F32)[:, None] * inv_freq[None]
    zero = jnp.zeros_like(ang_r)
    cos = jnp.concatenate([jnp.cos(ang_r)] * 2 + [jnp.cos(ang_c)] * 2, axis=-1)
    sin_lo = jnp.concatenate([-jnp.sin(ang_r), zero, -jnp.sin(ang_c), zero], axis=-1)
    sin_hi = jnp.concatenate([zero, jnp.sin(ang_r), zero, jnp.sin(ang_c)], axis=-1)
    rep = LANES // HEAD_DIM
    return tuple(jnp.tile(a, (1, rep)) for a in (cos, sin_lo, sin_hi))


def _project(x2d, bsz, s_len, g_mix, w_in, g_qnorm, g_knorm):
    t_total = x2d.shape[0]
    tm = TM_PROJ
    tiles_per_b = s_len // tm
    cos, slo, shi = _rope_tables(s_len)
    grp = jnp.arange(B_WIDTH) // HEAD_DIM
    ones = (grp[:, None] == grp[None, :]).astype(BF16)
    gq = jnp.tile(g_qnorm.astype(F32), B_HEADS)[None]
    gk = jnp.tile(g_knorm.astype(F32), B_KV_HEADS)[None]

    const = lambda t: (0, 0)
    per_batch_t = lambda t: (t // tiles_per_b, 0, t % tiles_per_b)
    per_batch_tiles = lambda t: (t // tiles_per_b, t % tiles_per_b, 0, 0)
    out_shape = (
        jax.ShapeDtypeStruct((bsz, A_WIDTH, s_len), BF16),
        jax.ShapeDtypeStruct((t_total, A_WIDTH), BF16),
        jax.ShapeDtypeStruct((bsz, s_len // TK_A, A_HEADS * (A_VDIM + ONES_ROWS), TK_A), BF16),
        jax.ShapeDtypeStruct((bsz, B_WIDTH, s_len), BF16),
        jax.ShapeDtypeStruct((B_KV_HEADS, t_total, HEAD_DIM), BF16),
        jax.ShapeDtypeStruct((bsz, s_len // TK_B, B_KV_HEADS * (HEAD_DIM + ONES_ROWS), TK_B), BF16),
    )
    return pl.pallas_call(
        functools.partial(_proj_kernel, tka=TK_A, tkb=TK_B),
        grid=(t_total // tm,),
        in_specs=[
            pl.BlockSpec((tm, D_MODEL), lambda t: (t, 0)),
            pl.BlockSpec((1, D_MODEL), const),
            pl.BlockSpec((D_MODEL, IN_COLS), const),
            pl.BlockSpec((B_WIDTH, B_WIDTH), const),
            pl.BlockSpec((1, B_WIDTH), const),
            pl.BlockSpec((1, B_KV_WIDTH), const),
            pl.BlockSpec((tm, LANES), lambda t: (t % tiles_per_b, 0)),
            pl.BlockSpec((tm, LANES), lambda t: (t % tiles_per_b, 0)),
            pl.BlockSpec((tm, LANES), lambda t: (t % tiles_per_b, 0)),
        ],
        out_specs=(
            pl.BlockSpec((1, A_WIDTH, tm), per_batch_t),
            pl.BlockSpec((tm, A_WIDTH), lambda t: (t, 0)),
            pl.BlockSpec((1, tm // TK_A, A_HEADS * (A_VDIM + ONES_ROWS), TK_A), per_batch_tiles),
            pl.BlockSpec((1, B_WIDTH, tm), per_batch_t),
            pl.BlockSpec((B_KV_HEADS, tm, HEAD_DIM), lambda t: (0, t, 0)),
            pl.BlockSpec((1, tm // TK_B, B_KV_HEADS * (HEAD_DIM + ONES_ROWS), TK_B), per_batch_tiles),
        ),
        out_shape=out_shape,
        compiler_params=pltpu.CompilerParams(
            dimension_semantics=("parallel",), vmem_limit_bytes=VMEM_LIMIT),
        name="proj",
    )(x2d, g_mix[None].astype(F32), w_in.astype(BF16), ones, gq, gk, cos, slo, shi)


def _col_max(m_run, s):
    return jnp.maximum(m_run, jnp.max(s, axis=0, keepdims=True))


def _accumulate(s, vt, m_prev, m_cur, acc_ref, idx):
    alpha = jnp.exp2(m_prev - m_cur)
    p = jnp.exp2(s - m_cur)
    acc_ref[idx] = alpha * acc_ref[idx] + jnp.dot(vt, p.astype(BF16), preferred_element_type=F32)


def _normalized(acc, dv):
    return acc[:dv] * (1.0 / acc[dv:dv + 1])


def _pipelined_sweep(n_tiles, score_stage, value_stage, s_bufs, m_init, unroll):
    assert unroll % 2 == 0
    m0 = score_stage(0, m_init, s_bufs[0], first=True)

    def step(i, parity, m_prev, m_cur):
        m_nxt = score_stage(i + 1, m_cur, s_bufs[1 - parity])
        value_stage(i, s_bufs[parity], m_prev, m_cur)
        return m_cur, m_nxt

    def body(t, carry):
        for u in range(unroll):
            carry = step(t * unroll + u, u % 2, *carry)
        return carry

    n_steps = n_tiles - 1
    trips = n_steps // unroll if n_steps >= 2 * unroll else 0
    carry = (m0, m0)
    if trips:
        carry = lax.fori_loop(0, trips, body, carry)
    for i in range(trips * unroll, n_steps):
        carry = step(i, i % 2, *carry)
    value_stage(n_steps, s_bufs[n_steps % 2], *carry)


def _attn_a_kernel(lq1_ref, lk1_ref, lq2_ref, lk2_ref, slope_ref, g_ref, qt_ref, qaugt_ref, k_ref,
                   kaug_ref, vt_ref, o_ref, w_ref, s0_ref, s1_ref, acc_ref, *, lam_init):
    tq = qt_ref.shape[-1]
    tk = vt_ref.shape[-1]
    nk = vt_ref.shape[1]
    q0 = pl.program_id(2) * tq
    kt_diag = q0 // tk

    qt = qt_ref[0]
    qaugt = qaugt_ref[0]
    zero = jnp.zeros((HEAD_DIM, tq), BF16)
    for c in range(2):
        rows = [qt[:HEAD_DIM], zero] if c == 0 else [zero, qt[HEAD_DIM:]]
        w_ref[0, c] = jnp.concatenate(rows + [qaugt], axis=0)
        w_ref[1, c] = jnp.concatenate(rows + [-qaugt], axis=0)

    acc_ref[...] = jnp.zeros(acc_ref.shape, F32)

    def tile_of(j):
        left = j - 1 < kt_diag
        kt = jnp.where(j == 0, kt_diag, jnp.where(left, j - 1, j))
        return kt, jnp.where(left, 0, 1)

    def score_stage(j, m_run, s_buf, first=False):
        kt, side = (kt_diag, 0) if first else tile_of(j)
        k0 = pl.multiple_of(kt * tk, tk)
        kcat = jnp.concatenate([k_ref[pl.ds(k0, tk), :], kaug_ref[0, pl.ds(k0, tk), :]], axis=1)
        if first:
            kpos = k0 + lax.broadcasted_iota(jnp.int32, (tk, tq), 0)
            qpos = q0 + lax.broadcasted_iota(jnp.int32, (tk, tq), 1)
            fix = jnp.maximum(kpos - qpos, 0).astype(F32) * (-2.0 * LOG2E * slope_ref[0, 0:1, 0:1])
        out = []
        for c in range(2):
            s = jnp.dot(kcat, w_ref[side, c], preferred_element_type=F32)
            if first:
                s = s + fix
            s_buf[c] = s
            out.append(_col_max(m_run[c], s))
        return tuple(out)

    def value_stage(j, s_buf, m_prev, m_cur):
        vt = vt_ref[0, tile_of(j)[0]]
        for c in range(2):
            _accumulate(s_buf[c], vt, m_prev[c], m_cur[c], acc_ref, c)

    m_init = (jnp.full((1, tq), NEG_BIG, F32),) * 2
    _pipelined_sweep(nk, score_stage, value_stage, (s0_ref, s1_ref), m_init, SWEEP_UNROLL_A)

    lam = (jnp.exp(jnp.sum(lq1_ref[...] * lk1_ref[...], axis=-1, keepdims=True))
           - jnp.exp(jnp.sum(lq2_ref[...] * lk2_ref[...], axis=-1, keepdims=True)) + lam_init)
    o_t = _normalized(acc_ref[0], A_VDIM) - lam * _normalized(acc_ref[1], A_VDIM)
    o = o_t.T
    o_ref[...] = (_rms(o, g_ref[...], SUBLN_EPS) * (1.0 - lam_init)).astype(BF16)


def _alibi_tables(s_len):
    pos = jnp.arange(s_len)
    parts = (((pos // POS_SPLIT) * POS_SPLIT).astype(F32), (pos % POS_SPLIT).astype(F32))
    slopes = jnp.asarray([2.0 ** (-8.0 * (h + 1) / A_HEADS) for h in range(A_HEADS)], F32)
    n_cols = 4 * len(LOG2E_PIECES)
    pad = jnp.zeros((A_HEADS, s_len, LANES - n_cols), F32)

    def per_head(cols):
        cols = [jnp.broadcast_to(c, (A_HEADS, s_len)) for c in cols]
        return jnp.concatenate([jnp.stack(cols, axis=-1), pad], axis=-1).astype(BF16)

    coef = [slopes[:, None] * c for c in LOG2E_PIECES]
    kaug = per_head([part[None] for part in parts for _ in coef] + coef + coef)
    qaug = per_head(coef + coef + [-part[None] for part in parts for _ in coef])
    slope_tab = jnp.broadcast_to(slopes[:, None, None], (A_HEADS, 8, LANES))
    return jnp.swapaxes(qaug, 1, 2), kaug, slope_tab


def _attention_a(qat, ka, vat, bsz, s_len, lq1, lk1, lq2, lk2, g_subln, lam_init):
    nq = s_len // TQ_A
    qaugt, kaug, slope_tab = _alibi_tables(s_len)
    vec = lambda a: a.astype(F32)[None]
    small = lambda b, h, i: (0, 0)
    return pl.pallas_call(
        functools.partial(_attn_a_kernel, lam_init=lam_init),
        grid=(bsz, A_HEADS, nq),
        in_specs=[
            pl.BlockSpec((1, HEAD_DIM), small),
            pl.BlockSpec((1, HEAD_DIM), small),
            pl.BlockSpec((1, HEAD_DIM), small),
            pl.BlockSpec((1, HEAD_DIM), small),
            pl.BlockSpec((1, 8, LANES), lambda b, h, i: (h, 0, 0)),
            pl.BlockSpec((1, A_VDIM), small),
            pl.BlockSpec((1, A_VDIM, TQ_A), lambda b, h, i: (b, h, i)),
            pl.BlockSpec((1, LANES, TQ_A), lambda b, h, i: (h, 0, i)),
            pl.BlockSpec((s_len, A_VDIM), lambda b, h, i: (b, h)),
            pl.BlockSpec((1, s_len, LANES), lambda b, h, i: (h, 0, 0)),
            pl.BlockSpec((1, s_len // TK_A, A_VDIM + ONES_ROWS, TK_A), lambda b, h, i: (b, 0, h, 0)),
        ],
        out_specs=pl.BlockSpec((TQ_A, A_VDIM), lambda b, h, i: (b * nq + i, h)),
        out_shape=jax.ShapeDtypeStruct((bsz * s_len, A_WIDTH), BF16),
        scratch_shapes=[
            pltpu.VMEM((2, 2, A_VDIM + LANES, TQ_A), BF16),
            pltpu.VMEM((2, TK_A, TQ_A), F32),
            pltpu.VMEM((2, TK_A, TQ_A), F32),
            pltpu.VMEM((2, A_VDIM + ONES_ROWS, TQ_A), F32),
        ],
        compiler_params=pltpu.CompilerParams(
            dimension_semantics=("parallel", "parallel", "arbitrary"), vmem_limit_bytes=VMEM_LIMIT),
        name="attn_a",
    )(vec(lq1), vec(lk1), vec(lq2), vec(lk2), slope_tab, vec(g_subln), qat, qaugt, ka, kaug, vat)


def _attn_b_kernel(qt_ref, k_ref, vt_ref, o_ref, w_ref, s0_ref, s1_ref, acc_ref):
    tq = qt_ref.shape[-1]
    tk = vt_ref.shape[-1]
    nk = vt_ref.shape[1]

    for h in range(B_GROUP):
        w_ref[:, h * tq:(h + 1) * tq] = qt_ref[0, h * HEAD_DIM:(h + 1) * HEAD_DIM, :]

    acc_ref[...] = jnp.zeros(acc_ref.shape, F32)

    def score_stage(j, m_run, s_buf, first=False):
        k0 = pl.multiple_of(j * tk, tk)
        s = jnp.dot(k_ref[0, pl.ds(k0, tk), :], w_ref[...], preferred_element_type=F32)
        s_buf[...] = s
        return _col_max(m_run, s)

    def value_stage(j, s_buf, m_prev, m_cur):
        _accumulate(s_buf[...], vt_ref[0, j], m_prev, m_cur, acc_ref, 0)

    m_init = jnp.full((1, B_GROUP * tq), NEG_BIG, F32)
    _pipelined_sweep(nk, score_stage, value_stage, (s0_ref, s1_ref), m_init, SWEEP_UNROLL_B)

    o_t = _normalized(acc_ref[0], HEAD_DIM)
    stacked = jnp.concatenate([o_t[:, h * tq:(h + 1) * tq] for h in range(B_GROUP)], axis=0)
    o_ref[...] = stacked.T.astype(BF16)


def _attention_b(qbt, kb, vbt, bsz, s_len):
    nq = s_len // TQ_B
    ncol = B_GROUP * TQ_B
    return pl.pallas_call(
        _attn_b_kernel,
        grid=(bsz, B_KV_HEADS, nq),
        in_specs=[
            pl.BlockSpec((1, B_GROUP * HEAD_DIM, TQ_B), lambda b, g, i: (b, g, i)),
            pl.BlockSpec((1, s_len, HEAD_DIM), lambda b, g, i: (g, b, 0)),
            pl.BlockSpec((1, s_len // TK_B, HEAD_DIM + ONES_ROWS, TK_B), lambda b, g, i: (b, 0, g, 0)),
        ],
        out_specs=pl.BlockSpec((TQ_B, B_GROUP * HEAD_DIM), lambda b, g, i: (b * nq + i, g)),
        out_shape=jax.ShapeDtypeStruct((bsz * s_len, B_WIDTH), BF16),
        scratch_shapes=[
            pltpu.VMEM((HEAD_DIM, ncol), BF16),
            pltpu.VMEM((TK_B, ncol), F32),
            pltpu.VMEM((TK_B, ncol), F32),
            pltpu.VMEM((1, HEAD_DIM + ONES_ROWS, ncol), F32),
        ],
        compiler_params=pltpu.CompilerParams(
            dimension_semantics=("parallel", "parallel", "arbitrary"), vmem_limit_bytes=VMEM_LIMIT),
        name="attn_b",
    )(qbt, kb, vbt)


def _post_kernel(x_ref, oa_ref, ob_ref, p_ref, wout_ref, gmlp_ref, w1_ref, w2_ref, gple_ref,
                 wg_ref, wp_ref, gfin_ref, y_ref, *, final_norm):
    mixed = jnp.concatenate([oa_ref[...], ob_ref[...]], axis=1)
    h = x_ref[...] + jnp.dot(mixed, wout_ref[...], preferred_element_type=F32)

    n2 = _rms(h, gmlp_ref[...], NORM_EPS).astype(BF16)
    for c in range(D_FF // FF_CHUNK):
        a = jnp.maximum(jnp.dot(n2, w1_ref[:, c * FF_CHUNK:(c + 1) * FF_CHUNK],
                                preferred_element_type=F32), 0.0)
        h = h + jnp.dot((a * a).astype(BF16), w2_ref[c * FF_CHUNK:(c + 1) * FF_CHUNK, :],
                        preferred_element_type=F32)

    n3 = _rms(h, gple_ref[...], NORM_EPS).astype(BF16)
    gate = jax.nn.sigmoid(jnp.dot(n3, wg_ref[...], preferred_element_type=F32))
    h = h + gate * jnp.dot(p_ref[...].astype(BF16), wp_ref[...], preferred_element_type=F32)
    if final_norm:
        h = _rms(h, gfin_ref[...], NORM_EPS)
    y_ref[...] = h


def _post(x2d, oa, ob, p2d, w_out, g_mlp, w_ff1, w_ff2, g_ple, w_gate, w_proj, g_final, final_norm):
    t_total = x2d.shape[0]
    tm = TM_POST
    const = lambda t: (0, 0)
    resident = functools.partial(pl.BlockSpec, index_map=const, pipeline_mode=pl.Buffered(1))
    row = lambda a: a.astype(F32)[None]
    return pl.pallas_call(
        functools.partial(_post_kernel, final_norm=final_norm),
        grid=(t_total // tm,),
        in_specs=[
            pl.BlockSpec((tm, D_MODEL), lambda t: (t, 0)),
            pl.BlockSpec((tm, A_WIDTH), lambda t: (t, 0)),
            pl.BlockSpec((tm, B_WIDTH), lambda t: (t, 0)),
            pl.BlockSpec((tm, PLE_DIM), lambda t: (t, 0)),
            resident((D_MODEL, D_MODEL)),
            pl.BlockSpec((1, D_MODEL), const),
            resident((D_MODEL, D_FF)),
            resident((D_FF, D_MODEL)),
            pl.BlockSpec((1, D_MODEL), const),
            resident((D_MODEL, D_MODEL)),
            resident((PLE_DIM, D_MODEL)),
            pl.BlockSpec((1, D_MODEL), const),
        ],
        out_specs=pl.BlockSpec((tm, D_MODEL), lambda t: (t, 0)),
        out_shape=jax.ShapeDtypeStruct((t_total, D_MODEL), F32),
        compiler_params=pltpu.CompilerParams(
            dimension_semantics=("parallel",), vmem_limit_bytes=VMEM_LIMIT),
        name="post",
    )(x2d, oa, ob, p2d, w_out.astype(BF16), row(g_mlp), w_ff1.astype(BF16), w_ff2.astype(BF16),
      row(g_ple), w_gate.astype(BF16), w_proj.astype(BF16), row(g_final))


def _trunk(x, p, w_in, g_mix, lambda_q1, lambda_k1, lambda_q2, lambda_k2, g_subln, g_qnorm, g_knorm,
           w_out, g_mlp, w_ff1, w_ff2, g_ple, w_ple_gate, w_ple_proj, g_final):
    bsz, s_len, _ = x.shape
    depth = w_in.shape[0]
    h2d = x.reshape(bsz * s_len, D_MODEL)
    for l in range(depth):
        qat, ka, vat, qbt, kb, vbt = _project(h2d, bsz, s_len, g_mix[l], w_in[l], g_qnorm[l], g_knorm[l])
        oa = _attention_a(qat, ka, vat, bsz, s_len, lambda_q1[l], lambda_k1[l], lambda_q2[l],
                          lambda_k2[l], g_subln[l], _lambda_init(l))
        ob = _attention_b(qbt, kb, vbt, bsz, s_len)
        h2d = _post(h2d, oa, ob, p[l].reshape(bsz * s_len, PLE_DIM), w_out[l], g_mlp[l], w_ff1[l],
                    w_ff2[l], g_ple[l], w_ple_gate[l], w_ple_proj[l], g_final,
                    final_norm=(l == depth - 1))
    return h2d.reshape(bsz, s_len, D_MODEL)


def kernel(x_prompt, x_sample, p_prompt, p_sample, w_in, g_mix, lambda_q1, lambda_k1, lambda_q2, lambda_k2,
           g_subln, g_qnorm, g_knorm, w_out, g_mlp, w_ff1, w_ff2, g_ple, w_ple_gate, w_ple_proj, g_final):
    weights = (w_in, g_mix, lambda_q1, lambda_k1, lambda_q2, lambda_k2, g_subln, g_qnorm, g_knorm,
               w_out, g_mlp, w_ff1, w_ff2, g_ple, w_ple_gate, w_ple_proj, g_final)
    return (_trunk(x_prompt, p_prompt, *weights), _trunk(x_sample, p_sample, *weights))
```

```python
import functools
import math

import jax
import jax.numpy as jnp
import ml_dtypes
import numpy as np
from jax import lax
from jax.experimental import pallas as pl
from jax.experimental.pallas import tpu as pltpu

D_MODEL = 1024
HEAD_DIM = 64
A_HEADS = 4
A_VDIM = 2 * HEAD_DIM
A_WIDTH = A_HEADS * A_VDIM
B_HEADS = 8
B_KV_HEADS = 2
B_GROUP = B_HEADS // B_KV_HEADS
B_WIDTH = B_HEADS * HEAD_DIM
B_KV_WIDTH = B_KV_HEADS * HEAD_DIM
IN_COLS = 3 * A_WIDTH + B_WIDTH + 2 * B_KV_WIDTH
D_FF = 4 * D_MODEL
PLE_DIM = 256
GRID_W = 64
ROPE_THETA = 10000.0
ROPE_HALF = HEAD_DIM // 2
ROPE_QUARTER = ROPE_HALF // 2
NORM_EPS = 1e-6
SUBLN_EPS = 1e-5
SCALE = HEAD_DIM ** -0.5


def _bf16_pieces(x, n):
    out = []
    for _ in range(n):
        piece = float(np.float32(x).astype(ml_dtypes.bfloat16))
        out.append(piece)
        x -= piece
    return out


LOG2E_PIECES = _bf16_pieces(math.log2(math.e), 3)
LOG2E = float(np.float32(sum(LOG2E_PIECES)))
Q_SCALE = SCALE * LOG2E
ONES_ROWS = 16

LANES = 128
POS_SPLIT = 128
NEG_BIG = -1e30

TM_PROJ = 512
TM_POST = 256
TQ_A = 256
TK_A = 512
TQ_B = 256
TK_B = 256
SWEEP_UNROLL_A = 6
SWEEP_UNROLL_B = 8
Q_SWEEPS = 2
FF_CHUNK = 1024
VMEM_LIMIT = 56 * 1024 * 1024

BF16 = jnp.bfloat16
F32 = jnp.float32


def _lambda_init(layer_idx):
    return 0.8 - 0.6 * math.exp(-0.3 * layer_idx)


def _rms(x, g, eps):
    return x * lax.rsqrt(jnp.mean(x * x, axis=-1, keepdims=True) + eps) * g


def _group_sumsq(y, ones):
    y2 = y * y
    hi = y2.astype(BF16)
    lo = (y2 - hi.astype(F32)).astype(BF16)
    return jnp.dot(jnp.concatenate([hi, lo], axis=1), jnp.concatenate([ones, ones], axis=0),
                   preferred_element_type=F32)


def _rope_chunk(y, cos, sin_lo, sin_hi):
    return (y * cos
            + pltpu.roll(y, LANES - ROPE_QUARTER, 1) * sin_lo
            + pltpu.roll(y, ROPE_QUARTER, 1) * sin_hi)


def _store_values_t(vt_ref, v, tk, dv):
    tm, width = v.shape
    blk = dv + ONES_ROWS
    ones = jnp.ones((ONES_ROWS, tk), BF16)
    for j in range(tm // tk):
        vt = v[j * tk:(j + 1) * tk, :].T.astype(BF16)
        for h in range(width // dv):
            vt_ref[0, j, h * blk:h * blk + dv, :] = vt[h * dv:(h + 1) * dv]
            vt_ref[0, j, h * blk + dv:(h + 1) * blk, :] = ones


def _proj_kernel(x_ref, gmix_ref, win_ref, ones_ref, gq_ref, gk_ref, cos_ref, slo_ref, shi_ref,
                 qat_ref, ka_ref, vat_ref, qbt_ref, kb_ref, vbt_ref, *, tka, tkb):
    x = x_ref[...]
    n = _rms(x, gmix_ref[...], NORM_EPS)
    z = jnp.dot(n.astype(BF16), win_ref[...], preferred_element_type=F32)
    tm = x.shape[0]

    c0 = 0
    qat_ref[0] = (z[:, c0:c0 + A_WIDTH] * Q_SCALE).T.astype(BF16)
    c0 += A_WIDTH
    ka_ref[...] = z[:, c0:c0 + A_WIDTH].astype(BF16)
    c0 += A_WIDTH
    va = z[:, c0:c0 + A_WIDTH]
    c0 += A_WIDTH
    qb = z[:, c0:c0 + B_WIDTH]
    c0 += B_WIDTH
    kb = z[:, c0:c0 + B_KV_WIDTH]
    c0 += B_KV_WIDTH
    vb = z[:, c0:c0 + B_KV_WIDTH]

    _store_values_t(vat_ref, va, tka, A_VDIM)
    _store_values_t(vbt_ref, vb, tkb, HEAD_DIM)

    ones = ones_ref[...]
    cos = cos_ref[...]
    slo = slo_ref[...]
    shi = shi_ref[...]

    qn = qb * lax.rsqrt(_group_sumsq(qb, ones) * (1.0 / HEAD_DIM) + NORM_EPS) * gq_ref[...]
    for j in range(B_WIDTH // LANES):
        rot = _rope_chunk(qn[:, j * LANES:(j + 1) * LANES], cos, slo, shi) * Q_SCALE
        qbt_ref[0, j * LANES:(j + 1) * LANES, :] = rot.T.astype(BF16)

    kn = kb * lax.rsqrt(_group_sumsq(kb, ones[:B_KV_WIDTH, :B_KV_WIDTH]) * (1.0 / HEAD_DIM)
                        + NORM_EPS) * gk_ref[...]
    rot = _rope_chunk(kn, cos, slo, shi)
    kb_ref[0] = rot[:, :HEAD_DIM].astype(BF16)
    kb_ref[1] = rot[:, HEAD_DIM:].astype(BF16)


def _rope_tables(s_len):
    t = jnp.arange(s_len)
    inv_freq = ROPE_THETA ** (-jnp.arange(0, ROPE_HALF, 2, dtype=F32) / ROPE_HALF)
    ang_r = (t // GRID_W).astype(F32)[:, None] * inv_freq[None]
    ang_c = (t % GRID_W).astype(F32)[:, None] * inv_freq[None]
    zero = jnp.zeros_like(ang_r)
    cos = jnp.concatenate([jnp.cos(ang_r)] * 2 + [jnp.cos(ang_c)] * 2, axis=-1)
    sin_lo = jnp.concatenate([-jnp.sin(ang_r), zero, -jnp.sin(ang_c), zero], axis=-1)
    sin_hi = jnp.concatenate([zero, jnp.sin(ang_r), zero, jnp.sin(ang_c)], axis=-1)
    rep = LANES // HEAD_DIM
    return tuple(jnp.tile(a, (1, rep)) for a in (cos, sin_lo, sin_hi))


def _project(x2d, bsz, s_len, g_mix, w_in, g_qnorm, g_knorm):
    t_total = x2d.shape[0]
    tm = TM_PROJ
    tiles_per_b = s_len // tm
    cos, slo, shi = _rope_tables(s_len)
    grp = jnp.arange(B_WIDTH) // HEAD_DIM
    ones = (grp[:, None] == grp[None, :]).astype(BF16)
    gq = jnp.tile(g_qnorm.astype(F32), B_HEADS)[None]
    gk = jnp.tile(g_knorm.astype(F32), B_KV_HEADS)[None]

    const = lambda t: (0, 0)
    per_batch_t = lambda t: (t // tiles_per_b, 0, t % tiles_per_b)
    per_batch_tiles = lambda t: (t // tiles_per_b, t % tiles_per_b, 0, 0)
    out_shape = (
        jax.ShapeDtypeStruct((bsz, A_WIDTH, s_len), BF16),
        jax.ShapeDtypeStruct((t_total, A_WIDTH), BF16),
        jax.ShapeDtypeStruct((bsz, s_len // TK_A, A_HEADS * (A_VDIM + ONES_ROWS), TK_A), BF16),
        jax.ShapeDtypeStruct((bsz, B_WIDTH, s_len), BF16),
        jax.ShapeDtypeStruct((B_KV_HEADS, t_total, HEAD_DIM), BF16),
        jax.ShapeDtypeStruct((bsz, s_len // TK_B, B_KV_HEADS * (HEAD_DIM + ONES_ROWS), TK_B), BF16),
    )
    return pl.pallas_call(
        functools.partial(_proj_kernel, tka=TK_A, tkb=TK_B),
        grid=(t_total // tm,),
        in_specs=[
            pl.BlockSpec((tm, D_MODEL), lambda t: (t, 0)),
            pl.BlockSpec((1, D_MODEL), const),
            pl.BlockSpec((D_MODEL, IN_COLS), const),
            pl.BlockSpec((B_WIDTH, B_WIDTH), const),
            pl.BlockSpec((1, B_WIDTH), const),
            pl.BlockSpec((1, B_KV_WIDTH), const),
            pl.BlockSpec((tm, LANES), lambda t: (t % tiles_per_b, 0)),
            pl.BlockSpec((tm, LANES), lambda t: (t % tiles_per_b, 0)),
            pl.BlockSpec((tm, LANES), lambda t: (t % tiles_per_b, 0)),
        ],
        out_specs=(
            pl.BlockSpec((1, A_WIDTH, tm), per_batch_t),
            pl.BlockSpec((tm, A_WIDTH), lambda t: (t, 0)),
            pl.BlockSpec((1, tm // TK_A, A_HEADS * (A_VDIM + ONES_ROWS), TK_A), per_batch_tiles),
            pl.BlockSpec((1, B_WIDTH, tm), per_batch_t),
            pl.BlockSpec((B_KV_HEADS, tm, HEAD_DIM), lambda t: (0, t, 0)),
            pl.BlockSpec((1, tm // TK_B, B_KV_HEADS * (HEAD_DIM + ONES_ROWS), TK_B), per_batch_tiles),
        ),
        out_shape=out_shape,
        compiler_params=pltpu.CompilerParams(
            dimension_semantics=("parallel",), vmem_limit_bytes=VMEM_LIMIT),
        name="proj",
    )(x2d, g_mix[None].astype(F32), w_in.astype(BF16), ones, gq, gk, cos, slo, shi)


def _col_max(m_run, s):
    return jnp.maximum(m_run, jnp.max(s, axis=0, keepdims=True))


def _accumulate(s, vt, m_prev, m_cur, acc_ref, idx):
    alpha = jnp.exp2(m_prev - m_cur)
    p = jnp.exp2(s - m_cur)
    acc_ref[idx] = alpha * acc_ref[idx] + jnp.dot(vt, p.astype(BF16), preferred_element_type=F32)


def _normalized(acc, dv):
    return acc[:dv] * (1.0 / acc[dv:dv + 1])


def _pipelined_sweeps(n_tiles, sweeps, s_bufs, unroll):
    assert unroll % 2 == 0 and n_tiles % 2 == 0
    n_steps = n_tiles - 1
    trips = n_steps // unroll if n_steps >= 2 * unroll else 0
    finish_previous = None
    for score_stage, value_stage, m_init in sweeps:
        m0 = score_stage(0, m_init, s_bufs[0], first=True)
        if finish_previous is not None:
            finish_previous()

        def step(i, parity, m_prev, m_cur, score_stage=score_stage, value_stage=value_stage):
            m_nxt = score_stage(i + 1, m_cur, s_bufs[1 - parity])
            value_stage(i, s_bufs[parity], m_prev, m_cur)
            return m_cur, m_nxt

        def body(t, carry, step=step):
            for u in range(unroll):
                carry = step(t * unroll + u, u % 2, *carry)
            return carry

        carry = (m0, m0)
        if trips:
            carry = lax.fori_loop(0, trips, body, carry)
        for i in range(trips * unroll, n_steps):
            carry = step(i, i % 2, *carry)
        finish_previous = functools.partial(value_stage, n_steps, s_bufs[n_steps % 2], *carry)
    finish_previous()


def _attn_a_kernel(lq1_ref, lk1_ref, lq2_ref, lk2_ref, slope_ref, g_ref, qt_ref, qaugt_ref, k_ref,
                   kaug_ref, vt_ref, o_ref, w_ref, s0_ref, s1_ref, acc_ref, *, lam_init):
    tq = w_ref.shape[-1]
    n_sweeps = w_ref.shape[0]
    tk = vt_ref.shape[-1]
    nk = vt_ref.shape[1]

    acc_ref[...] = jnp.zeros(acc_ref.shape, F32)

    def make_sweep(t):
        q0 = (pl.program_id(2) * n_sweeps + t) * tq
        kt_diag = q0 // tk

        qt = qt_ref[0, :, t * tq:(t + 1) * tq]
        qaugt = qaugt_ref[0, :, t * tq:(t + 1) * tq]
        zero = jnp.zeros((HEAD_DIM, tq), BF16)
        for c in range(2):
            rows = [qt[:HEAD_DIM], zero] if c == 0 else [zero, qt[HEAD_DIM:]]
            w_ref[t, 0, c] = jnp.concatenate(rows + [qaugt], axis=0)
            w_ref[t, 1, c] = jnp.concatenate(rows + [-qaugt], axis=0)

        def tile_of(j):
            left = j - 1 < kt_diag
            kt = jnp.where(j == 0, kt_diag, jnp.where(left, j - 1, j))
            return kt, jnp.where(left, 0, 1)

        def score_stage(j, m_run, s_buf, first=False):
            kt, side = (kt_diag, 0) if first else tile_of(j)
            k0 = pl.multiple_of(kt * tk, tk)
            kcat = jnp.concatenate([k_ref[pl.ds(k0, tk), :], kaug_ref[0, pl.ds(k0, tk), :]], axis=1)
            if first:
                kpos = k0 + lax.broadcasted_iota(jnp.int32, (tk, tq), 0)
                qpos = q0 + lax.broadcasted_iota(jnp.int32, (tk, tq), 1)
                fix = jnp.maximum(kpos - qpos, 0).astype(F32) * (-2.0 * LOG2E * slope_ref[0, 0:1, 0:1])
            out = []
            for c in range(2):
                s = jnp.dot(kcat, w_ref[t, side, c], preferred_element_type=F32)
                if first:
                    s = s + fix
                s_buf[c] = s
                out.append(_col_max(m_run[c], s))
            return tuple(out)

        def value_stage(j, s_buf, m_prev, m_cur):
            vt = vt_ref[0, tile_of(j)[0]]
            for c in range(2):
                _accumulate(s_buf[c], vt, m_prev[c], m_cur[c], acc_ref, (t, c))

        return score_stage, value_stage, (jnp.full((1, tq), NEG_BIG, F32),) * 2

    _pipelined_sweeps(nk, [make_sweep(t) for t in range(n_sweeps)], (s0_ref, s1_ref), SWEEP_UNROLL_A)

    lam = (jnp.exp(jnp.sum(lq1_ref[...] * lk1_ref[...], axis=-1, keepdims=True))
           - jnp.exp(jnp.sum(lq2_ref[...] * lk2_ref[...], axis=-1, keepdims=True)) + lam_init)
    for t in range(n_sweeps):
        o_t = _normalized(acc_ref[t, 0], A_VDIM) - lam * _normalized(acc_ref[t, 1], A_VDIM)
        o = o_t.T
        o_ref[t * tq:(t + 1) * tq, :] = (_rms(o, g_ref[...], SUBLN_EPS) * (1.0 - lam_init)).astype(BF16)


def _alibi_tables(s_len):
    n = len(LOG2E_PIECES)
    slopes = np.asarray([2.0 ** (-8.0 * (h + 1) / A_HEADS) for h in range(A_HEADS)], np.float32)
    coef = np.zeros((A_HEADS, LANES), np.float32)
    coef[:, :4 * n] = slopes[:, None] * np.tile(np.asarray(LOG2E_PIECES, np.float32), 4)[None]

    def table(shape, col_axis, pos_axis, coef_b, q_side):
        col = lax.broadcasted_iota(jnp.int32, shape, col_axis)
        pos = lax.broadcasted_iota(jnp.int32, shape, pos_axis)
        hi = ((pos // POS_SPLIT) * POS_SPLIT).astype(F32)
        lo = (pos % POS_SPLIT).astype(F32)
        if q_side:
            val = jnp.where(col < 2 * n, coef_b, jnp.where(col < 3 * n, -hi, jnp.where(col < 4 * n, -lo, 0.0)))
        else:
            val = jnp.where(col < n, hi, jnp.where(col < 2 * n, lo, coef_b))
        return val.astype(BF16)

    kaug = table((A_HEADS, s_len, LANES), 2, 1, jnp.asarray(coef)[:, None, :], False)
    qaugt = table((A_HEADS, LANES, s_len), 1, 2, jnp.asarray(coef)[:, :, None], True)
    slope_tab = jnp.asarray(np.broadcast_to(slopes[:, None, None], (A_HEADS, 8, LANES)))
    return qaugt, kaug, slope_tab


def _attention_a(qat, ka, vat, bsz, s_len, lq1, lk1, lq2, lk2, g_subln, lam_init):
    tqs = TQ_A * Q_SWEEPS
    nq = s_len // tqs
    qaugt, kaug, slope_tab = _alibi_tables(s_len)
    vec = lambda a: a.astype(F32)[None]
    small = lambda b, h, i: (0, 0)
    return pl.pallas_call(
        functools.partial(_attn_a_kernel, lam_init=lam_init),
        grid=(bsz, A_HEADS, nq),
        in_specs=[
            pl.BlockSpec((1, HEAD_DIM), small),
            pl.BlockSpec((1, HEAD_DIM), small),
            pl.BlockSpec((1, HEAD_DIM), small),
            pl.BlockSpec((1, HEAD_DIM), small),
            pl.BlockSpec((1, 8, LANES), lambda b, h, i: (h, 0, 0)),
            pl.BlockSpec((1, A_VDIM), small),
            pl.BlockSpec((1, A_VDIM, tqs), lambda b, h, i: (b, h, i)),
            pl.BlockSpec((1, LANES, tqs), lambda b, h, i: (h, 0, i)),
            pl.BlockSpec((s_len, A_VDIM), lambda b, h, i: (b, h)),
            pl.BlockSpec((1, s_len, LANES), lambda b, h, i: (h, 0, 0)),
            pl.BlockSpec((1, s_len // TK_A, A_VDIM + ONES_ROWS, TK_A), lambda b, h, i: (b, 0, h, 0)),
        ],
        out_specs=pl.BlockSpec((tqs, A_VDIM), lambda b, h, i: (b * nq + i, h)),
        out_shape=jax.ShapeDtypeStruct((bsz * s_len, A_WIDTH), BF16),
        scratch_shapes=[
            pltpu.VMEM((Q_SWEEPS, 2, 2, A_VDIM + LANES, TQ_A), BF16),
            pltpu.VMEM((2, TK_A, TQ_A), F32),
            pltpu.VMEM((2, TK_A, TQ_A), F32),
            pltpu.VMEM((Q_SWEEPS, 2, A_VDIM + ONES_ROWS, TQ_A), F32),
        ],
        compiler_params=pltpu.CompilerParams(
            dimension_semantics=("parallel", "parallel", "arbitrary"), vmem_limit_bytes=VMEM_LIMIT),
        name="attn_a",
    )(vec(lq1), vec(lk1), vec(lq2), vec(lk2), slope_tab, vec(g_subln), qat, qaugt, ka, kaug, vat)


def _attn_b_kernel(qt_ref, k_ref, vt_ref, o_ref, w_ref, s0_ref, s1_ref, acc_ref):
    n_sweeps = w_ref.shape[0]
    tq = w_ref.shape[-1] // B_GROUP
    tk = vt_ref.shape[-1]
    nk = vt_ref.shape[1]

    acc_ref[...] = jnp.zeros(acc_ref.shape, F32)

    def make_sweep(t):
        for h in range(B_GROUP):
            w_ref[t, :, h * tq:(h + 1) * tq] = qt_ref[0, h * HEAD_DIM:(h + 1) * HEAD_DIM, t * tq:(t + 1) * tq]

        def score_stage(j, m_run, s_buf, first=False):
            k0 = pl.multiple_of(j * tk, tk)
            s = jnp.dot(k_ref[0, pl.ds(k0, tk), :], w_ref[t], preferred_element_type=F32)
            s_buf[...] = s
            return _col_max(m_run, s)

        def value_stage(j, s_buf, m_prev, m_cur):
            _accumulate(s_buf[...], vt_ref[0, j], m_prev, m_cur, acc_ref, t)

        return score_stage, value_stage, jnp.full((1, B_GROUP * tq), NEG_BIG, F32)

    _pipelined_sweeps(nk, [make_sweep(t) for t in range(n_sweeps)], (s0_ref, s1_ref), SWEEP_UNROLL_B)

    for t in range(n_sweeps):
        o_t = _normalized(acc_ref[t], HEAD_DIM)
        stacked = jnp.concatenate([o_t[:, h * tq:(h + 1) * tq] for h in range(B_GROUP)], axis=0)
        o_ref[t * tq:(t + 1) * tq, :] = stacked.T.astype(BF16)


def _attention_b(qbt, kb, vbt, bsz, s_len):
    tqs = TQ_B * Q_SWEEPS
    nq = s_len // tqs
    ncol = B_GROUP * TQ_B
    return pl.pallas_call(
        _attn_b_kernel,
        grid=(bsz, B_KV_HEADS, nq),
        in_specs=[
            pl.BlockSpec((1, B_GROUP * HEAD_DIM, tqs), lambda b, g, i: (b, g, i)),
            pl.BlockSpec((1, s_len, HEAD_DIM), lambda b, g, i: (g, b, 0)),
            pl.BlockSpec((1, s_len // TK_B, HEAD_DIM + ONES_ROWS, TK_B), lambda b, g, i: (b, 0, g, 0)),
        ],
        out_specs=pl.BlockSpec((tqs, B_GROUP * HEAD_DIM), lambda b, g, i: (b * nq + i, g)),
        out_shape=jax.ShapeDtypeStruct((bsz * s_len, B_WIDTH), BF16),
        scratch_shapes=[
            pltpu.VMEM((Q_SWEEPS, HEAD_DIM, ncol), BF16),
            pltpu.VMEM((TK_B, ncol), F32),
            pltpu.VMEM((TK_B, ncol), F32),
            pltpu.VMEM((Q_SWEEPS, HEAD_DIM + ONES_ROWS, ncol), F32),
        ],
        compiler_params=pltpu.CompilerParams(
            dimension_semantics=("parallel", "parallel", "arbitrary"), vmem_limit_bytes=VMEM_LIMIT),
        name="attn_b",
    )(qbt, kb, vbt)


def _post_kernel(x_ref, oa_ref, ob_ref, p_ref, wout_ref, gmlp_ref, w1_ref, w2_ref, gple_ref,
                 wg_ref, wp_ref, gfin_ref, y_ref, *, final_norm):
    mixed = jnp.concatenate([oa_ref[...], ob_ref[...]], axis=1)
    h = x_ref[...] + jnp.dot(mixed, wout_ref[...], preferred_element_type=F32)

    n2 = _rms(h, gmlp_ref[...], NORM_EPS).astype(BF16)
    for c in range(D_FF // FF_CHUNK):
        a = jnp.maximum(jnp.dot(n2, w1_ref[:, c * FF_CHUNK:(c + 1) * FF_CHUNK],
                                preferred_element_type=F32), 0.0)
        h = h + jnp.dot((a * a).astype(BF16), w2_ref[c * FF_CHUNK:(c + 1) * FF_CHUNK, :],
                        preferred_element_type=F32)

    n3 = _rms(h, gple_ref[...], NORM_EPS).astype(BF16)
    gate = jax.nn.sigmoid(jnp.dot(n3, wg_ref[...], preferred_element_type=F32))
    h = h + gate * jnp.dot(p_ref[...].astype(BF16), wp_ref[...], preferred_element_type=F32)
    if final_norm:
        h = _rms(h, gfin_ref[...], NORM_EPS)
    y_ref[...] = h


def _post(x2d, oa, ob, p2d, w_out, g_mlp, w_ff1, w_ff2, g_ple, w_gate, w_proj, g_final, final_norm):
    t_total = x2d.shape[0]
    tm = TM_POST
    const = lambda t: (0, 0)
    resident = functools.partial(pl.BlockSpec, index_map=const, pipeline_mode=pl.Buffered(1))
    row = lambda a: a.astype(F32)[None]
    return pl.pallas_call(
        functools.partial(_post_kernel, final_norm=final_norm),
        grid=(t_total // tm,),
        in_specs=[
            pl.BlockSpec((tm, D_MODEL), lambda t: (t, 0)),
            pl.BlockSpec((tm, A_WIDTH), lambda t: (t, 0)),
            pl.BlockSpec((tm, B_WIDTH), lambda t: (t, 0)),
            pl.BlockSpec((tm, PLE_DIM), lambda t: (t, 0)),
            resident((D_MODEL, D_MODEL)),
            pl.BlockSpec((1, D_MODEL), const),
            resident((D_MODEL, D_FF)),
            resident((D_FF, D_MODEL)),
            pl.BlockSpec((1, D_MODEL), const),
            resident((D_MODEL, D_MODEL)),
            resident((PLE_DIM, D_MODEL)),
            pl.BlockSpec((1, D_MODEL), const),
        ],
        out_specs=pl.BlockSpec((tm, D_MODEL), lambda t: (t, 0)),
        out_shape=jax.ShapeDtypeStruct((t_total, D_MODEL), F32),
        compiler_params=pltpu.CompilerParams(
            dimension_semantics=("parallel",), vmem_limit_bytes=VMEM_LIMIT),
        name="post",
    )(x2d, oa, ob, p2d, w_out.astype(BF16), row(g_mlp), w_ff1.astype(BF16), w_ff2.astype(BF16),
      row(g_ple), w_gate.astype(BF16), w_proj.astype(BF16), row(g_final))


def _trunk(x, p, w_in, g_mix, lambda_q1, lambda_k1, lambda_q2, lambda_k2, g_subln, g_qnorm, g_knorm,
           w_out, g_mlp, w_ff1, w_ff2, g_ple, w_ple_gate, w_ple_proj, g_final):
    bsz, s_len, _ = x.shape
    depth = w_in.shape[0]
    h2d = x.reshape(bsz * s_len, D_MODEL)
    for l in range(depth):
        qat, ka, vat, qbt, kb, vbt = _project(h2d, bsz, s_len, g_mix[l], w_in[l], g_qnorm[l], g_knorm[l])
        oa = _attention_a(qat, ka, vat, bsz, s_len, lambda_q1[l], lambda_k1[l], lambda_q2[l],
                          lambda_k2[l], g_subln[l], _lambda_init(l))
        ob = _attention_b(qbt, kb, vbt, bsz, s_len)
        h2d = _post(h2d, oa, ob, p[l].reshape(bsz * s_len, PLE_DIM), w_out[l], g_mlp[l], w_ff1[l],
                    w_ff2[l], g_ple[l], w_ple_gate[l], w_ple_proj[l], g_final,
                    final_norm=(l == depth - 1))
    return h2d.reshape(bsz, s_len, D_MODEL)


def kernel(x_prompt, x_sample, p_prompt, p_sample, w_in, g_mix, lambda_q1, lambda_k1, lambda_q2, lambda_k2,
           g_subln, g_qnorm, g_knorm, w_out, g_mlp, w_ff1, w_ff2, g_ple, w_ple_gate, w_ple_proj, g_final):
    weights = (w_in, g_mix, lambda_q1, lambda_k1, lambda_q2, lambda_k2, g_subln, g_qnorm, g_knorm,
               w_out, g_mlp, w_ff1, w_ff2, g_ple, w_ple_gate, w_ple_proj, g_final)
    return (_trunk(x_prompt, p_prompt, *weights), _trunk(x_sample, p_sample, *weights))
```

```python
import functools
import math

import jax
import jax.numpy as jnp
import ml_dtypes
import numpy as np
from jax import lax
from jax.experimental import pallas as pl
from jax.experimental.pallas import tpu as pltpu

D_MODEL = 1024
HEAD_DIM = 64
A_HEADS = 4
A_VDIM = 2 * HEAD_DIM
A_WIDTH = A_HEADS * A_VDIM
B_HEADS = 8
B_KV_HEADS = 2
B_GROUP = B_HEADS // B_KV_HEADS
B_WIDTH = B_HEADS * HEAD_DIM
B_KV_WIDTH = B_KV_HEADS * HEAD_DIM
IN_COLS = 3 * A_WIDTH + B_WIDTH + 2 * B_KV_WIDTH
D_FF = 4 * D_MODEL
PLE_DIM = 256
GRID_W = 64
ROPE_THETA = 10000.0
ROPE_HALF = HEAD_DIM // 2
ROPE_QUARTER = ROPE_HALF // 2
NORM_EPS = 1e-6
SUBLN_EPS = 1e-5
SCALE = HEAD_DIM ** -0.5


def _bf16_pieces(x, n):
    out = []
    for _ in range(n):
        piece = float(np.float32(x).astype(ml_dtypes.bfloat16))
        out.append(piece)
        x -= piece
    return out


LOG2E_PIECES = _bf16_pieces(math.log2(math.e), 3)
LOG2E = float(np.float32(sum(LOG2E_PIECES)))
Q_SCALE = SCALE * LOG2E
ONES_ROWS = 16

LANES = 128
POS_SPLIT = 128
NEG_BIG = -1e30

TM_PROJ = 512
TM_POST = 512
TQ_A = 256
TK_A = 512
TQ_B = 256
TK_B = 256
SWEEP_UNROLL_A = 6
SWEEP_UNROLL_B = 8
STAGES_PER_STEP = 32


def _q_sweeps(n_key_tiles):
    return max(2, min(4, STAGES_PER_STEP // n_key_tiles))
FF_CHUNK = 1024
VMEM_LIMIT = 56 * 1024 * 1024

BF16 = jnp.bfloat16
F32 = jnp.float32


def _lambda_init(layer_idx):
    return 0.8 - 0.6 * math.exp(-0.3 * layer_idx)


def _rms(x, g, eps):
    return x * lax.rsqrt(jnp.mean(x * x, axis=-1, keepdims=True) + eps) * g


def _group_sumsq(y, ones):
    y2 = y * y
    hi = y2.astype(BF16)
    lo = (y2 - hi.astype(F32)).astype(BF16)
    return jnp.dot(jnp.concatenate([hi, lo], axis=1), jnp.concatenate([ones, ones], axis=0),
                   preferred_element_type=F32)


def _rope_chunk(y, cos, sin_lo, sin_hi):
    return (y * cos
            + pltpu.roll(y, LANES - ROPE_QUARTER, 1) * sin_lo
            + pltpu.roll(y, ROPE_QUARTER, 1) * sin_hi)


def _store_values_t(vt_ref, v, tk, dv):
    tm, width = v.shape
    blk = dv + ONES_ROWS
    ones = jnp.ones((ONES_ROWS, tk), BF16)
    for j in range(tm // tk):
        vt = v[j * tk:(j + 1) * tk, :].T.astype(BF16)
        for h in range(width // dv):
            vt_ref[0, j, h * blk:h * blk + dv, :] = vt[h * dv:(h + 1) * dv]
            vt_ref[0, j, h * blk + dv:(h + 1) * blk, :] = ones


def _proj_kernel(x_ref, gmix_ref, win_ref, ones_ref, gq_ref, gk_ref, cos_ref, slo_ref, shi_ref,
                 qat_ref, ka_ref, vat_ref, qbt_ref, kb_ref, vbt_ref, *, tka, tkb):
    x = x_ref[...]
    n = _rms(x, gmix_ref[...], NORM_EPS)
    z = jnp.dot(n.astype(BF16), win_ref[...], preferred_element_type=F32)
    tm = x.shape[0]

    c0 = 0
    qat_ref[0] = (z[:, c0:c0 + A_WIDTH] * Q_SCALE).T.astype(BF16)
    c0 += A_WIDTH
    ka_ref[...] = z[:, c0:c0 + A_WIDTH].astype(BF16)
    c0 += A_WIDTH
    va = z[:, c0:c0 + A_WIDTH]
    c0 += A_WIDTH
    qb = z[:, c0:c0 + B_WIDTH]
    c0 += B_WIDTH
    kb = z[:, c0:c0 + B_KV_WIDTH]
    c0 += B_KV_WIDTH
    vb = z[:, c0:c0 + B_KV_WIDTH]

    _store_values_t(vat_ref, va, tka, A_VDIM)
    _store_values_t(vbt_ref, vb, tkb, HEAD_DIM)

    ones = ones_ref[...]
    cos = cos_ref[...]
    slo = slo_ref[...]
    shi = shi_ref[...]

    qn = qb * lax.rsqrt(_group_sumsq(qb, ones) * (1.0 / HEAD_DIM) + NORM_EPS) * gq_ref[...]
    for j in range(B_WIDTH // LANES):
        rot = _rope_chunk(qn[:, j * LANES:(j + 1) * LANES], cos, slo, shi) * Q_SCALE
        qbt_ref[0, j * LANES:(j + 1) * LANES, :] = rot.T.astype(BF16)

    kn = kb * lax.rsqrt(_group_sumsq(kb, ones[:B_KV_WIDTH, :B_KV_WIDTH]) * (1.0 / HEAD_DIM)
                        + NORM_EPS) * gk_ref[...]
    rot = _rope_chunk(kn, cos, slo, shi)
    kb_ref[0] = rot[:, :HEAD_DIM].astype(BF16)
    kb_ref[1] = rot[:, HEAD_DIM:].astype(BF16)


def _rope_tables(s_len):
    t = jnp.arange(s_len)
    inv_freq = ROPE_THETA ** (-jnp.arange(0, ROPE_HALF, 2, dtype=F32) / ROPE_HALF)
    ang_r = (t // GRID_W).astype(F32)[:, None] * inv_freq[None]
    ang_c = (t % GRID_W).astype(F32)[:, None] * inv_freq[None]
    zero = jnp.zeros_like(ang_r)
    cos = jnp.concatenate([jnp.cos(ang_r)] * 2 + [jnp.cos(ang_c)] * 2, axis=-1)
    sin_lo = jnp.concatenate([-jnp.sin(ang_r), zero, -jnp.sin(ang_c), zero], axis=-1)
    sin_hi = jnp.concatenate([zero, jnp.sin(ang_r), zero, jnp.sin(ang_c)], axis=-1)
    rep = LANES // HEAD_DIM
    return tuple(jnp.tile(a, (1, rep)) for a in (cos, sin_lo, sin_hi))


def _project(x2d, bsz, s_len, g_mix, w_in, g_qnorm, g_knorm):
    t_total = x2d.shape[0]
    tm = TM_PROJ
    tiles_per_b = s_len // tm
    cos, slo, shi = _rope_tables(s_len)
    grp = jnp.arange(B_WIDTH) // HEAD_DIM
    ones = (grp[:, None] == grp[None, :]).astype(BF16)
    gq = jnp.tile(g_qnorm.astype(F32), B_HEADS)[None]
    gk = jnp.tile(g_knorm.astype(F32), B_KV_HEADS)[None]

    const = lambda t: (0, 0)
    per_batch_t = lambda t: (t // tiles_per_b, 0, t % tiles_per_b)
    per_batch_tiles = lambda t: (t // tiles_per_b, t % tiles_per_b, 0, 0)
    out_shape = (
        jax.ShapeDtypeStruct((bsz, A_WIDTH, s_len), BF16),
        jax.ShapeDtypeStruct((t_total, A_WIDTH), BF16),
        jax.ShapeDtypeStruct((bsz, s_len // TK_A, A_HEADS * (A_VDIM + ONES_ROWS), TK_A), BF16),
        jax.ShapeDtypeStruct((bsz, B_WIDTH, s_len), BF16),
        jax.ShapeDtypeStruct((B_KV_HEADS, t_total, HEAD_DIM), BF16),
        jax.ShapeDtypeStruct((bsz, s_len // TK_B, B_KV_HEADS * (HEAD_DIM + ONES_ROWS), TK_B), BF16),
    )
    return pl.pallas_call(
        functools.partial(_proj_kernel, tka=TK_A, tkb=TK_B),
        grid=(t_total // tm,),
        in_specs=[
            pl.BlockSpec((tm, D_MODEL), lambda t: (t, 0)),
            pl.BlockSpec((1, D_MODEL), const),
            pl.BlockSpec((D_MODEL, IN_COLS), const),
            pl.BlockSpec((B_WIDTH, B_WIDTH), const),
            pl.BlockSpec((1, B_WIDTH), const),
            pl.BlockSpec((1, B_KV_WIDTH), const),
            pl.BlockSpec((tm, LANES), lambda t: (t % tiles_per_b, 0)),
            pl.BlockSpec((tm, LANES), lambda t: (t % tiles_per_b, 0)),
            pl.BlockSpec((tm, LANES), lambda t: (t % tiles_per_b, 0)),
        ],
        out_specs=(
            pl.BlockSpec((1, A_WIDTH, tm), per_batch_t),
            pl.BlockSpec((tm, A_WIDTH), lambda t: (t, 0)),
            pl.BlockSpec((1, tm // TK_A, A_HEADS * (A_VDIM + ONES_ROWS), TK_A), per_batch_tiles),
            pl.BlockSpec((1, B_WIDTH, tm), per_batch_t),
            pl.BlockSpec((B_KV_HEADS, tm, HEAD_DIM), lambda t: (0, t, 0)),
            pl.BlockSpec((1, tm // TK_B, B_KV_HEADS * (HEAD_DIM + ONES_ROWS), TK_B), per_batch_tiles),
        ),
        out_shape=out_shape,
        compiler_params=pltpu.CompilerParams(
            dimension_semantics=("parallel",), vmem_limit_bytes=VMEM_LIMIT),
        name="proj",
    )(x2d, g_mix[None].astype(F32), w_in.astype(BF16), ones, gq, gk, cos, slo, shi)


def _col_max(m_run, s):
    return jnp.maximum(m_run, jnp.max(s, axis=0, keepdims=True))


def _accumulate(s, vt, m_prev, m_cur, acc_ref, idx):
    alpha = jnp.exp2(m_prev - m_cur)
    p = jnp.exp2(s - m_cur)
    acc_ref[idx] = alpha * acc_ref[idx] + jnp.dot(vt, p.astype(BF16), preferred_element_type=F32)


def _normalized(acc, dv):
    return acc[:dv] * (1.0 / acc[dv:dv + 1])


def _pipelined_sweeps(n_tiles, sweeps, s_bufs, unroll):
    assert unroll % 2 == 0 and n_tiles % 2 == 0
    n_steps = n_tiles - 1
    trips = n_steps // unroll if n_steps >= 2 * unroll else 0
    finish_previous = None
    for score_stage, value_stage, m_init in sweeps:
        m0 = score_stage(0, m_init, s_bufs[0], first=True)
        if finish_previous is not None:
            finish_previous()

        def step(i, parity, m_prev, m_cur, score_stage=score_stage, value_stage=value_stage):
            m_nxt = score_stage(i + 1, m_cur, s_bufs[1 - parity])
            value_stage(i, s_bufs[parity], m_prev, m_cur)
            return m_cur, m_nxt

        def body(t, carry, step=step):
            for u in range(unroll):
                carry = step(t * unroll + u, u % 2, *carry)
            return carry

        carry = (m0, m0)
        if trips:
            carry = lax.fori_loop(0, trips, body, carry)
        for i in range(trips * unroll, n_steps):
            carry = step(i, i % 2, *carry)
        finish_previous = functools.partial(value_stage, n_steps, s_bufs[n_steps % 2], *carry)
    finish_previous()


def _attn_a_kernel(lq1_ref, lk1_ref, lq2_ref, lk2_ref, slope_ref, g_ref, qt_ref, qaugt_ref, k_ref,
                   kaug_ref, vt_ref, o_ref, w_ref, s0_ref, s1_ref, acc_ref, *, lam_init):
    tq = w_ref.shape[-1]
    n_sweeps = w_ref.shape[0]
    tk = vt_ref.shape[-1]
    nk = vt_ref.shape[1]

    acc_ref[...] = jnp.zeros(acc_ref.shape, F32)

    def make_sweep(t):
        q0 = (pl.program_id(2) * n_sweeps + t) * tq
        kt_diag = q0 // tk

        qt = qt_ref[0, :, t * tq:(t + 1) * tq]
        qaugt = qaugt_ref[0, :, t * tq:(t + 1) * tq]
        zero = jnp.zeros((HEAD_DIM, tq), BF16)
        for c in range(2):
            rows = [qt[:HEAD_DIM], zero] if c == 0 else [zero, qt[HEAD_DIM:]]
            w_ref[t, 0, c] = jnp.concatenate(rows + [qaugt], axis=0)
            w_ref[t, 1, c] = jnp.concatenate(rows + [-qaugt], axis=0)

        def tile_of(j):
            left = j - 1 < kt_diag
            kt = jnp.where(j == 0, kt_diag, jnp.where(left, j - 1, j))
            return kt, jnp.where(left, 0, 1)

        def score_stage(j, m_run, s_buf, first=False):
            kt, side = (kt_diag, 0) if first else tile_of(j)
            k0 = pl.multiple_of(kt * tk, tk)
            kcat = jnp.concatenate([k_ref[pl.ds(k0, tk), :], kaug_ref[0, pl.ds(k0, tk), :]], axis=1)
            if first:
                kpos = k0 + lax.broadcasted_iota(jnp.int32, (tk, tq), 0)
                qpos = q0 + lax.broadcasted_iota(jnp.int32, (tk, tq), 1)
                fix = jnp.maximum(kpos - qpos, 0).astype(F32) * (-2.0 * LOG2E * slope_ref[0, 0:1, 0:1])
            out = []
            for c in range(2):
                s = jnp.dot(kcat, w_ref[t, side, c], preferred_element_type=F32)
                if first:
                    s = s + fix
                s_buf[c] = s
                out.append(_col_max(m_run[c], s))
            return tuple(out)

        def value_stage(j, s_buf, m_prev, m_cur):
            vt = vt_ref[0, tile_of(j)[0]]
            for c in range(2):
                _accumulate(s_buf[c], vt, m_prev[c], m_cur[c], acc_ref, (t, c))

        return score_stage, value_stage, (jnp.full((1, tq), NEG_BIG, F32),) * 2

    _pipelined_sweeps(nk, [make_sweep(t) for t in range(n_sweeps)], (s0_ref, s1_ref), SWEEP_UNROLL_A)

    lam = (jnp.exp(jnp.sum(lq1_ref[...] * lk1_ref[...], axis=-1, keepdims=True))
           - jnp.exp(jnp.sum(lq2_ref[...] * lk2_ref[...], axis=-1, keepdims=True)) + lam_init)
    for t in range(n_sweeps):
        o_t = _normalized(acc_ref[t, 0], A_VDIM) - lam * _normalized(acc_ref[t, 1], A_VDIM)
        o = o_t.T
        o_ref[t * tq:(t + 1) * tq, :] = (_rms(o, g_ref[...], SUBLN_EPS) * (1.0 - lam_init)).astype(BF16)


def _alibi_tables(s_len):
    n = len(LOG2E_PIECES)
    slopes = np.asarray([2.0 ** (-8.0 * (h + 1) / A_HEADS) for h in range(A_HEADS)], np.float32)
    coef = np.zeros((A_HEADS, LANES), np.float32)
    coef[:, :4 * n] = slopes[:, None] * np.tile(np.asarray(LOG2E_PIECES, np.float32), 4)[None]

    def table(shape, col_axis, pos_axis, coef_b, q_side):
        col = lax.broadcasted_iota(jnp.int32, shape, col_axis)
        pos = lax.broadcasted_iota(jnp.int32, shape, pos_axis)
        hi = ((pos // POS_SPLIT) * POS_SPLIT).astype(F32)
        lo = (pos % POS_SPLIT).astype(F32)
        if q_side:
            val = jnp.where(col < 2 * n, coef_b, jnp.where(col < 3 * n, -hi, jnp.where(col < 4 * n, -lo, 0.0)))
        else:
            val = jnp.where(col < n, hi, jnp.where(col < 2 * n, lo, coef_b))
        return val.astype(BF16)

    kaug = table((A_HEADS, s_len, LANES), 2, 1, jnp.asarray(coef)[:, None, :], False)
    qaugt = table((A_HEADS, LANES, s_len), 1, 2, jnp.asarray(coef)[:, :, None], True)
    slope_tab = jnp.asarray(np.broadcast_to(slopes[:, None, None], (A_HEADS, 8, LANES)))
    return qaugt, kaug, slope_tab


def _attention_a(qat, ka, vat, bsz, s_len, lq1, lk1, lq2, lk2, g_subln, lam_init):
    n_sweeps = _q_sweeps(s_len // TK_A)
    tqs = TQ_A * n_sweeps
    nq = s_len // tqs
    qaugt, kaug, slope_tab = _alibi_tables(s_len)
    vec = lambda a: a.astype(F32)[None]
    small = lambda b, h, i: (0, 0)
    return pl.pallas_call(
        functools.partial(_attn_a_kernel, lam_init=lam_init),
        grid=(bsz, A_HEADS, nq),
        in_specs=[
            pl.BlockSpec((1, HEAD_DIM), small),
            pl.BlockSpec((1, HEAD_DIM), small),
            pl.BlockSpec((1, HEAD_DIM), small),
            pl.BlockSpec((1, HEAD_DIM), small),
            pl.BlockSpec((1, 8, LANES), lambda b, h, i: (h, 0, 0)),
            pl.BlockSpec((1, A_VDIM), small),
            pl.BlockSpec((1, A_VDIM, tqs), lambda b, h, i: (b, h, i)),
            pl.BlockSpec((1, LANES, tqs), lambda b, h, i: (h, 0, i)),
            pl.BlockSpec((s_len, A_VDIM), lambda b, h, i: (b, h)),
            pl.BlockSpec((1, s_len, LANES), lambda b, h, i: (h, 0, 0)),
            pl.BlockSpec((1, s_len // TK_A, A_VDIM + ONES_ROWS, TK_A), lambda b, h, i: (b, 0, h, 0)),
        ],
        out_specs=pl.BlockSpec((tqs, A_VDIM), lambda b, h, i: (b * nq + i, h)),
        out_shape=jax.ShapeDtypeStruct((bsz * s_len, A_WIDTH), BF16),
        scratch_shapes=[
            pltpu.VMEM((n_sweeps, 2, 2, A_VDIM + LANES, TQ_A), BF16),
            pltpu.VMEM((2, TK_A, TQ_A), F32),
            pltpu.VMEM((2, TK_A, TQ_A), F32),
            pltpu.VMEM((n_sweeps, 2, A_VDIM + ONES_ROWS, TQ_A), F32),
        ],
        compiler_params=pltpu.CompilerParams(
            dimension_semantics=("parallel", "parallel", "arbitrary"), vmem_limit_bytes=VMEM_LIMIT),
        name="attn_a",
    )(vec(lq1), vec(lk1), vec(lq2), vec(lk2), slope_tab, vec(g_subln), qat, qaugt, ka, kaug, vat)


def _attn_b_kernel(qt_ref, k_ref, vt_ref, o_ref, w_ref, s0_ref, s1_ref, acc_ref):
    n_sweeps = w_ref.shape[0]
    tq = w_ref.shape[-1] // B_GROUP
    tk = vt_ref.shape[-1]
    nk = vt_ref.shape[1]

    acc_ref[...] = jnp.zeros(acc_ref.shape, F32)

    def make_sweep(t):
        for h in range(B_GROUP):
            w_ref[t, :, h * tq:(h + 1) * tq] = qt_ref[0, h * HEAD_DIM:(h + 1) * HEAD_DIM, t * tq:(t + 1) * tq]

        def score_stage(j, m_run, s_buf, first=False):
            k0 = pl.multiple_of(j * tk, tk)
            s = jnp.dot(k_ref[0, pl.ds(k0, tk), :], w_ref[t], preferred_element_type=F32)
            s_buf[...] = s
            return _col_max(m_run, s)

        def value_stage(j, s_buf, m_prev, m_cur):
            _accumulate(s_buf[...], vt_ref[0, j], m_prev, m_cur, acc_ref, t)

        return score_stage, value_stage, jnp.full((1, B_GROUP * tq), NEG_BIG, F32)

    _pipelined_sweeps(nk, [make_sweep(t) for t in range(n_sweeps)], (s0_ref, s1_ref), SWEEP_UNROLL_B)

    for t in range(n_sweeps):
        o_t = _normalized(acc_ref[t], HEAD_DIM)
        stacked = jnp.concatenate([o_t[:, h * tq:(h + 1) * tq] for h in range(B_GROUP)], axis=0)
        o_ref[t * tq:(t + 1) * tq, :] = stacked.T.astype(BF16)


def _attention_b(qbt, kb, vbt, bsz, s_len):
    n_sweeps = _q_sweeps(s_len // TK_B)
    tqs = TQ_B * n_sweeps
    nq = s_len // tqs
    ncol = B_GROUP * TQ_B
    return pl.pallas_call(
        _attn_b_kernel,
        grid=(bsz, B_KV_HEADS, nq),
        in_specs=[
            pl.BlockSpec((1, B_GROUP * HEAD_DIM, tqs), lambda b, g, i: (b, g, i)),
            pl.BlockSpec((1, s_len, HEAD_DIM), lambda b, g, i: (g, b, 0)),
            pl.BlockSpec((1, s_len // TK_B, HEAD_DIM + ONES_ROWS, TK_B), lambda b, g, i: (b, 0, g, 0)),
        ],
        out_specs=pl.BlockSpec((tqs, B_GROUP * HEAD_DIM), lambda b, g, i: (b * nq + i, g)),
        out_shape=jax.ShapeDtypeStruct((bsz * s_len, B_WIDTH), BF16),
        scratch_shapes=[
            pltpu.VMEM((n_sweeps, HEAD_DIM, ncol), BF16),
            pltpu.VMEM((TK_B, ncol), F32),
            pltpu.VMEM((TK_B, ncol), F32),
            pltpu.VMEM((n_sweeps, HEAD_DIM + ONES_ROWS, ncol), F32),
        ],
        compiler_params=pltpu.CompilerParams(
            dimension_semantics=("parallel", "parallel", "arbitrary"), vmem_limit_bytes=VMEM_LIMIT),
        name="attn_b",
    )(qbt, kb, vbt)


def _post_kernel(x_ref, oa_ref, ob_ref, p_ref, wout_ref, gmlp_ref, w1_ref, w2_ref, gple_ref,
                 wg_ref, wp_ref, gfin_ref, y_ref, *, final_norm):
    mixed = jnp.concatenate([oa_ref[...], ob_ref[...]], axis=1)
    h = x_ref[...] + jnp.dot(mixed, wout_ref[...], preferred_element_type=F32)

    n2 = _rms(h, gmlp_ref[...], NORM_EPS).astype(BF16)
    for c in range(D_FF // FF_CHUNK):
        a = jnp.maximum(jnp.dot(n2, w1_ref[:, c * FF_CHUNK:(c + 1) * FF_CHUNK],
                                preferred_element_type=F32), 0.0)
        h = h + jnp.dot((a * a).astype(BF16), w2_ref[c * FF_CHUNK:(c + 1) * FF_CHUNK, :],
                        preferred_element_type=F32)

    n3 = _rms(h, gple_ref[...], NORM_EPS).astype(BF16)
    gate = jax.nn.sigmoid(jnp.dot(n3, wg_ref[...], preferred_element_type=F32))
    h = h + gate * jnp.dot(p_ref[...].astype(BF16), wp_ref[...], preferred_element_type=F32)
    if final_norm:
        h = _rms(h, gfin_ref[...], NORM_EPS)
    y_ref[...] = h


def _post(x2d, oa, ob, p2d, w_out, g_mlp, w_ff1, w_ff2, g_ple, w_gate, w_proj, g_final, final_norm):
    t_total = x2d.shape[0]
    tm = TM_POST
    const = lambda t: (0, 0)
    resident = functools.partial(pl.BlockSpec, index_map=const, pipeline_mode=pl.Buffered(1))
    row = lambda a: a.astype(F32)[None]
    return pl.pallas_call(
        functools.partial(_post_kernel, final_norm=final_norm),
        grid=(t_total // tm,),
        in_specs=[
            pl.BlockSpec((tm, D_MODEL), lambda t: (t, 0)),
            pl.BlockSpec((tm, A_WIDTH), lambda t: (t, 0)),
            pl.BlockSpec((tm, B_WIDTH), lambda t: (t, 0)),
            pl.BlockSpec((tm, PLE_DIM), lambda t: (t, 0)),
            resident((D_MODEL, D_MODEL)),
            pl.BlockSpec((1, D_MODEL), const),
            resident((D_MODEL, D_FF)),
            resident((D_FF, D_MODEL)),
            pl.BlockSpec((1, D_MODEL), const),
            resident((D_MODEL, D_MODEL)),
            resident((PLE_DIM, D_MODEL)),
            pl.BlockSpec((1, D_MODEL), const),
        ],
        out_specs=pl.BlockSpec((tm, D_MODEL), lambda t: (t, 0)),
        out_shape=jax.ShapeDtypeStruct((t_total, D_MODEL), F32),
        compiler_params=pltpu.CompilerParams(
            dimension_semantics=("parallel",), vmem_limit_bytes=VMEM_LIMIT),
        name="post",
    )(x2d, oa, ob, p2d, w_out.astype(BF16), row(g_mlp), w_ff1.astype(BF16), w_ff2.astype(BF16),
      row(g_ple), w_gate.astype(BF16), w_proj.astype(BF16), row(g_final))


def _trunk(x, p, w_in, g_mix, lambda_q1, lambda_k1, lambda_q2, lambda_k2, g_subln, g_qnorm, g_knorm,
           w_out, g_mlp, w_ff1, w_ff2, g_ple, w_ple_gate, w_ple_proj, g_final):
    bsz, s_len, _ = x.shape
    depth = w_in.shape[0]
    h2d = x.reshape(bsz * s_len, D_MODEL)
    for l in range(depth):
        qat, ka, vat, qbt, kb, vbt = _project(h2d, bsz, s_len, g_mix[l], w_in[l], g_qnorm[l], g_knorm[l])
        oa = _attention_a(qat, ka, vat, bsz, s_len, lambda_q1[l], lambda_k1[l], lambda_q2[l],
                          lambda_k2[l], g_subln[l], _lambda_init(l))
        ob = _attention_b(qbt, kb, vbt, bsz, s_len)
        h2d = _post(h2d, oa, ob, p[l].reshape(bsz * s_len, PLE_DIM), w_out[l], g_mlp[l], w_ff1[l],
                    w_ff2[l], g_ple[l], w_ple_gate[l], w_ple_proj[l], g_final,
                    final_norm=(l == depth - 1))
    return h2d.reshape(bsz, s_len, D_MODEL)


def kernel(x_prompt, x_sample, p_prompt, p_sample, w_in, g_mix, lambda_q1, lambda_k1, lambda_q2, lambda_k2,
           g_subln, g_qnorm, g_knorm, w_out, g_mlp, w_ff1, w_ff2, g_ple, w_ple_gate, w_ple_proj, g_final):
    weights = (w_in, g_mix, lambda_q1, lambda_k1, lambda_q2, lambda_k2, g_subln, g_qnorm, g_knorm,
               w_out, g_mlp, w_ff1, w_ff2, g_ple, w_ple_gate, w_ple_proj, g_final)
    return (_trunk(x_prompt, p_prompt, *weights), _trunk(x_sample, p_sample, *weights))
```
